```python
import math
import jax
import jax.numpy as jnp
from jax import lax
import numpy as np

D_MODEL = 1024
BATCH = 1
SEQ = 16384
DEPTH = 2
DEC_BATCH = 32
DEC_SEQ = 1
PAST_LEN = 16384
PAGE_SIZE = 128

HEAD_DIM = 64
H_A = 8
H_B = 8
W_A = H_A * HEAD_DIM
W_B = H_B * HEAD_DIM
IDX_HEADS = 8
IDX_DIM = 32
TOPK_MAX = 256
N_BUCKETS = 32
MAX_DIST = 128
D_FF = 2816
CONV_W = 3
Q_BLOCK = 128
FORGET_BIAS = 3.0
EPS = 1e-6
NEG = -1e30
IN_SIZES = (W_A, W_A, W_A, IDX_HEADS * IDX_DIM, IDX_DIM, IDX_HEADS, W_B, W_B, W_B, H_B, D_MODEL, D_MODEL)
IN_COLS = sum(IN_SIZES)

kernel_name = 'hybrid_dsa_fox_convffn_step'


def rmsnorm(x, g):
    xf = x.astype(jnp.float32)
    y = xf * lax.rsqrt(jnp.mean(xf * xf, axis=-1, keepdims=True) + EPS)
    return (y * g.astype(jnp.float32)).astype(x.dtype)


def t5_bucket(dist):
    max_exact = N_BUCKETS // 2
    d = jnp.maximum(dist, 0)
    df = jnp.maximum(d.astype(jnp.float32), 1.0)
    large = max_exact + (jnp.log(df / max_exact) / math.log(MAX_DIST / max_exact)
                         * (N_BUCKETS - max_exact)).astype(jnp.int32)
    return jnp.where(d < max_exact, d, jnp.minimum(large, N_BUCKETS - 1))


def project_in(h, w_in, b_f):
    B, T = h.shape[:2]
    z = jnp.einsum('btd,dc->btc', h, w_in)
    cuts, acc = [], 0
    for s in IN_SIZES[:-1]:
        acc += s
        cuts.append(acc)
    q_a, k_a, v_a, q_i, k_i, w_i, q_b, k_b, v_b, f_b, g_a, g_b = jnp.split(z, cuts, axis=-1)
    hd = lambda t, n: t.reshape(B, T, n, HEAD_DIM)
    return {
        'q_a': hd(q_a, H_A), 'k_a': hd(k_a, H_A), 'v_a': hd(v_a, H_A),
        'q_idx': q_i.reshape(B, T, IDX_HEADS, IDX_DIM), 'k_idx': k_i, 'w_idx': w_i,
        'q_b': hd(q_b, H_B), 'k_b': hd(k_b, H_B), 'v_b': hd(v_b, H_B),
        'logf': jax.nn.log_sigmoid(f_b.astype(jnp.float32) + b_f.astype(jnp.float32)),
        'g_a': g_a, 'g_b': g_b,
    }


def indexer_scores(q_idx, w_idx, k_idx):
    s = jnp.einsum('bthd,bsd->bths', q_idx, k_idx, preferred_element_type=jnp.float32) * (IDX_DIM ** -0.5)
    return jnp.einsum('bths,bth->bts', jax.nn.relu(s), w_idx.astype(jnp.float32) * (IDX_HEADS ** -0.5))


def select_keys(q_idx, w_idx, k_idx, qpos, topk):
    score = indexer_scores(q_idx, w_idx, k_idx)
    kpos = jnp.arange(k_idx.shape[1], dtype=jnp.int32)
    score = jnp.where(kpos[None, None, :] <= qpos[None, :, None], score, -jnp.inf)
    _, idx = lax.top_k(score, topk)
    return idx, idx <= qpos[None, :, None]


def sparse_attend(q, k_sel, v_sel, bias, valid):
    logits = jnp.einsum('bthd,btkhd->bthk', q, k_sel, preferred_element_type=jnp.float32) * (HEAD_DIM ** -0.5)
    logits = logits + jnp.swapaxes(bias, 2, 3).astype(jnp.float32)
    logits = jnp.where(valid[:, :, None, :], logits, NEG)
    p = jax.nn.softmax(logits, axis=-1)
    return jnp.einsum('bthk,btkhd->bthd', p.astype(v_sel.dtype), v_sel)


def fox_attend(q, k, v, c_q, c_k, qpos):
    logits = jnp.einsum('bthd,bshd->bhts', q, k, preferred_element_type=jnp.float32) * (HEAD_DIM ** -0.5)
    logits = logits + jnp.swapaxes(c_q, 1, 2)[..., :, None] - jnp.swapaxes(c_k, 1, 2)[..., None, :]
    kpos = jnp.arange(k.shape[1], dtype=jnp.int32)
    logits = jnp.where(kpos[None, None, None, :] <= qpos[None, None, :, None], logits, NEG)
    p = jax.nn.softmax(logits, axis=-1)
    return jnp.einsum('bhts,bshd->bthd', p.astype(v.dtype), v)


def gather_rows(rows, idx):
    return jax.vmap(lambda r, i: r[i])(rows, idx)


def gather_pages(pool, page_table):
    DB, NP = page_table.shape
    return pool[page_table].reshape((DB, NP * PAGE_SIZE) + pool.shape[2:])


def gather_selected(pool, new, page_table, idx):
    DB, NP = page_table.shape
    P = NP * PAGE_SIZE
    T = new.shape[1]
    past = jnp.minimum(idx, P - 1)
    phys = jnp.take_along_axis(page_table, (past // PAGE_SIZE).reshape(DB, -1), axis=1).reshape(idx.shape)
    rows_past = pool[phys, past % PAGE_SIZE]
    rows_new = gather_rows(new, jnp.clip(idx - P, 0, T - 1))
    is_new = (idx >= P).reshape(idx.shape + (1,) * (rows_past.ndim - idx.ndim))
    return jnp.where(is_new, rows_new.astype(rows_past.dtype), rows_past)


def dsa_prompt(p, rel_bias):
    q, k, v = p['q_a'], p['k_a'], p['v_a']
    B, S = q.shape[:2]
    topk = min(TOPK_MAX, S // 4)

    def block(i):
        t0 = i * Q_BLOCK
        qpos = t0 + jnp.arange(Q_BLOCK, dtype=jnp.int32)
        qb = lax.dynamic_slice_in_dim(q, t0, Q_BLOCK, axis=1)
        qi = lax.dynamic_slice_in_dim(p['q_idx'], t0, Q_BLOCK, axis=1)
        wi = lax.dynamic_slice_in_dim(p['w_idx'], t0, Q_BLOCK, axis=1)
        idx, valid = select_keys(qi, wi, p['k_idx'], qpos, topk)
        bias = rel_bias[t5_bucket(qpos[None, :, None] - idx)]
        return sparse_attend(qb, gather_rows(k, idx), gather_rows(v, idx), bias, valid)

    out = lax.map(block, jnp.arange(S // Q_BLOCK, dtype=jnp.int32))
    return jnp.moveaxis(out, 0, 1).reshape(B, S, H_A, HEAD_DIM)


def dsa_sample(p, cache_k, cache_v, cache_kidx, page_table, rel_bias):
    q = p['q_a']
    T = q.shape[1]
    P = page_table.shape[1] * PAGE_SIZE
    topk = min(TOPK_MAX, (P + T) // 4)
    kidx_all = jnp.concatenate([gather_pages(cache_kidx, page_table),
                                p['k_idx'].astype(cache_kidx.dtype)], axis=1)
    qpos = P + jnp.arange(T, dtype=jnp.int32)
    idx, valid = select_keys(p['q_idx'], p['w_idx'], kidx_all, qpos, topk)
    k_sel = gather_selected(cache_k, p['k_a'], page_table, idx)
    v_sel = gather_selected(cache_v, p['v_a'], page_table, idx)
    bias = rel_bias[t5_bucket(qpos[None, :, None] - idx)]
    return sparse_attend(q, k_sel, v_sel, bias, valid)


def fox_prompt(p):
    q, k, v = p['q_b'], p['k_b'], p['v_b']
    B, S = q.shape[:2]
    c = jnp.cumsum(p['logf'], axis=1)

    def block(i):
        t0 = i * Q_BLOCK
        qpos = t0 + jnp.arange(Q_BLOCK, dtype=jnp.int32)
        qb = lax.dynamic_slice_in_dim(q, t0, Q_BLOCK, axis=1)
        cb = lax.dynamic_slice_in_dim(c, t0, Q_BLOCK, axis=1)
        return fox_attend(qb, k, v, cb, c, qpos)

    out = lax.map(block, jnp.arange(S // Q_BLOCK, dtype=jnp.int32))
    return jnp.moveaxis(out, 0, 1).reshape(B, S, H_B, HEAD_DIM)


def fox_sample(p, cache_k, cache_v, cache_logf, page_table):
    q = p['q_b']
    T = q.shape[1]
    P = page_table.shape[1] * PAGE_SIZE
    k_all = jnp.concatenate([gather_pages(cache_k, page_table), p['k_b'].astype(cache_k.dtype)], axis=1)
    v_all = jnp.concatenate([gather_pages(cache_v, page_table), p['v_b'].astype(cache_v.dtype)], axis=1)
    logf_all = jnp.concatenate([gather_pages(cache_logf, page_table).astype(jnp.float32), p['logf']], axis=1)
    c = jnp.cumsum(logf_all, axis=1)
    qpos = P + jnp.arange(T, dtype=jnp.int32)
    return fox_attend(q, k_all, v_all, c[:, P:], c, qpos)


def merge_branches(o_a, o_b, g_a, g_b, w_a_up, w_b_up, w_o):
    B, T = o_a.shape[:2]
    u_a = jnp.einsum('btc,cd->btd', o_a.reshape(B, T, W_A), w_a_up)
    u_b = jnp.einsum('btc,cd->btd', o_b.reshape(B, T, W_B), w_b_up)
    merged = jax.nn.sigmoid(g_a) * u_a + jax.nn.sigmoid(g_b) * u_b
    return jnp.einsum('btd,de->bte', merged, w_o)


def conv_ffn(h, state, w_up, conv_w, conv_b, w_down):
    T = h.shape[1]
    u = jnp.einsum('btd,df->btf', h, w_up)
    ext = jnp.concatenate([state.astype(u.dtype), u], axis=1)
    y = conv_b + sum(conv_w[j] * ext[:, j:j + T] for j in range(CONV_W))
    a, g = jnp.split(y, 2, axis=-1)
    out = jnp.einsum('btf,fd->btd', jax.nn.silu(g) * a, w_down)
    return out, ext[:, T:]


def setup_inputs(seed: int = 0) -> dict:
    key = jax.random.key(seed)
    ks = jax.random.split(key, 24)
    n_pages = PAST_LEN // PAGE_SIZE
    n_used = DEC_BATCH * n_pages
    n_pool = n_used + n_used // 4
    nrm = lambda k, shape, scale=1.0: scale * jax.random.normal(k, shape, jnp.float32)
    return {
        'x_prompt': nrm(ks[0], (BATCH, SEQ, D_MODEL)),
        'x_sample': nrm(ks[1], (DEC_BATCH, DEC_SEQ, D_MODEL)),
        'cache_k_a': nrm(ks[2], (DEPTH, n_pool, PAGE_SIZE, H_A, HEAD_DIM)),
        'cache_v_a': nrm(ks[3], (DEPTH, n_pool, PAGE_SIZE, H_A, HEAD_DIM)),
        'cache_kidx': nrm(ks[4], (DEPTH, n_pool, PAGE_SIZE, IDX_DIM)),
        'cache_k_b': nrm(ks[5], (DEPTH, n_pool, PAGE_SIZE, H_B, HEAD_DIM)),
        'cache_v_b': nrm(ks[6], (DEPTH, n_pool, PAGE_SIZE, H_B, HEAD_DIM)),
        'cache_logf_b': jax.nn.log_sigmoid(FORGET_BIAS + nrm(ks[7], (DEPTH, n_pool, PAGE_SIZE, H_B))),
        'state_conv': nrm(ks[8], (DEPTH, DEC_BATCH, CONV_W - 1, 2 * D_FF)),
        'page_table': jax.random.permutation(ks[9], n_pool)[:n_used].reshape(DEC_BATCH, n_pages).astype(jnp.int32),
        'rel_bias': nrm(ks[10], (N_BUCKETS, H_A), 0.5),
        'attn_norm': 1.0 + nrm(ks[11], (DEPTH, D_MODEL), 0.01),
        'w_in': nrm(ks[12], (DEPTH, D_MODEL, IN_COLS), D_MODEL ** -0.5),
        'b_forget': FORGET_BIAS + nrm(ks[13], (DEPTH, H_B), 0.1),
        'w_a_up': nrm(ks[14], (DEPTH, W_A, D_MODEL), W_A ** -0.5),
        'w_b_up': nrm(ks[15], (DEPTH, W_B, D_MODEL), W_B ** -0.5),
        'w_o': nrm(ks[16], (DEPTH, D_MODEL, D_MODEL), D_MODEL ** -0.5),
        'ffn_norm': 1.0 + nrm(ks[17], (DEPTH, D_MODEL), 0.01),
        'w_up': nrm(ks[18], (DEPTH, D_MODEL, 2 * D_FF), D_MODEL ** -0.5),
        'conv_w': nrm(ks[19], (DEPTH, CONV_W, 2 * D_FF), CONV_W ** -0.5),
        'conv_b': nrm(ks[20], (DEPTH, 2 * D_FF), 0.01),
        'w_down': nrm(ks[21], (DEPTH, D_FF, D_MODEL), D_FF ** -0.5),
        'final_norm': 1.0 + nrm(ks[22], (D_MODEL,), 0.01),
    }


def reference(x_prompt, x_sample, cache_k_a, cache_v_a, cache_kidx, cache_k_b, cache_v_b, cache_logf_b,
              state_conv, page_table, rel_bias, attn_norm, w_in, b_forget, w_a_up, w_b_up, w_o,
              ffn_norm, w_up, conv_w, conv_b, w_down, final_norm):
    xp, xs = x_prompt, x_sample
    new = [[] for _ in range(14)]
    for l in range(DEPTH):
        pp = project_in(rmsnorm(xp, attn_norm[l]), w_in[l], b_forget[l])
        ps = project_in(rmsnorm(xs, attn_norm[l]), w_in[l], b_forget[l])
        oa_p = dsa_prompt(pp, rel_bias)
        ob_p = fox_prompt(pp)
        oa_s = dsa_sample(ps, cache_k_a[l], cache_v_a[l], cache_kidx[l], page_table, rel_bias)
        ob_s = fox_sample(ps, cache_k_b[l], cache_v_b[l], cache_logf_b[l], page_table)
        xp = xp + merge_branches(oa_p, ob_p, pp['g_a'], pp['g_b'], w_a_up[l], w_b_up[l], w_o[l])
        xs = xs + merge_branches(oa_s, ob_s, ps['g_a'], ps['g_b'], w_a_up[l], w_b_up[l], w_o[l])
        zero_state = jnp.zeros((xp.shape[0], CONV_W - 1, 2 * D_FF), xp.dtype)
        f_p, conv_p = conv_ffn(rmsnorm(xp, ffn_norm[l]), zero_state, w_up[l], conv_w[l], conv_b[l], w_down[l])
        f_s, conv_s = conv_ffn(rmsnorm(xs, ffn_norm[l]), state_conv[l], w_up[l], conv_w[l], conv_b[l], w_down[l])
        xp = xp + f_p
        xs = xs + f_s
        rows = (pp['k_a'], ps['k_a'], pp['v_a'], ps['v_a'], pp['k_idx'], ps['k_idx'],
                pp['k_b'], ps['k_b'], pp['v_b'], ps['v_b'], pp['logf'], ps['logf'], conv_p, conv_s)
        for lst, r in zip(new, rows):
            lst.append(r)
    y_prompt = rmsnorm(xp, final_norm)
    y_sample = rmsnorm(xs, final_norm)
    (k_a_p, k_a_s, v_a_p, v_a_s, kidx_p, kidx_s, k_b_p, k_b_s, v_b_p, v_b_s,
     logf_p, logf_s, conv_p_all, conv_s_all) = [jnp.stack(n) for n in new]
    return (y_prompt, y_sample, k_a_p, k_a_s, v_a_p, v_a_s, kidx_p, kidx_s, k_b_p, k_b_s,
            v_b_p, v_b_s, logf_p, logf_s, conv_p_all, conv_s_all)
```

```python
import functools
import math

import numpy as np
import jax
import jax.numpy as jnp
from jax import lax
from jax.experimental import pallas as pl
from jax.experimental.pallas import tpu as pltpu

HEAD_DIM = 64
N_HEADS = 8
IDX_HEADS = 8
IDX_DIM = 32
TOPK_MAX = 256
N_BUCKETS = 32
MAX_DIST = 128
PAGE_SIZE = 128
FORGET_EPS = 1e-6
NEG = -1e30
INT_MIN = -2 ** 31
LANES = 128
VMEM_LIMIT = 56 * 1024 * 1024

_BF = jnp.bfloat16
_F32 = jnp.float32


def _cparams(sem):
    return pltpu.CompilerParams(dimension_semantics=sem, vmem_limit_bytes=VMEM_LIMIT)


def _split3(x):
    hi = x.astype(_BF)
    r1 = x - hi.astype(_F32)
    mid = r1.astype(_BF)
    lo = (r1 - mid.astype(_F32)).astype(_BF)
    return hi, mid, lo


def _dot3(x, b):
    hi, mid, lo = _split3(x)
    d = lambda a: jnp.dot(a, b, preferred_element_type=_F32)
    return d(hi) + d(mid) + d(lo)


def _rms(x, g):
    var = jnp.mean(x * x, axis=-1, keepdims=True)
    return x * lax.rsqrt(var + FORGET_EPS) * g


def _log_sigmoid(x):
    return -(jnp.maximum(-x, 0.0) + jnp.log1p(jnp.exp(-jnp.abs(x))))


def _dot_nt(a, b):
    return lax.dot_general(a, b, (((1,), (1,)), ((), ())), preferred_element_type=_F32)


_NAT = (("k_a", 512), ("v_a", 512), ("k_b", 512), ("v_b", 512), ("g_a", 1024), ("g_b", 1024), ("k_i", 128))
_TRN = (("q_a", 512), ("q_b", 512), ("v_a", 512), ("v_b", 512), ("q_i", 256), ("w_i", 16), ("f_b", 16))


def _proj_prompt_kernel(x_ref, g_ref, wn_ref, wt_ref, bf_ref,
                        ka_ref, va_ref, kb_ref, vb_ref, ga_ref, gb_ref, ki_ref, kab_ref,
                        qaT_ref, qbT_ref, vaT_ref, vbT_ref, qiT_ref, wiT_ref, lfT_ref):
    h = _rms(x_ref[...], g_ref[...]).astype(_BF)
    nat_refs = (ka_ref, va_ref, kb_ref, vb_ref, ga_ref, gb_ref, ki_ref)
    off = 0
    for (name, n), ref in zip(_NAT, nat_refs):
        z = jnp.dot(h, wn_ref[:, off:off + n], preferred_element_type=_F32)
        ref[...] = z
        if name == "k_a":
            kab_ref[...] = z.astype(_BF)
        off += n
    trn_refs = (qaT_ref, qbT_ref, vaT_ref, vbT_ref, qiT_ref, wiT_ref, lfT_ref)
    off = 0
    for (name, n), ref in zip(_TRN, trn_refs):
        z = _dot_nt(wt_ref[off:off + n, :], h)[:ref.shape[0]]
        if name == "f_b":
            z = _log_sigmoid(z + bf_ref[...])
        ref[...] = z.astype(ref.dtype)
        off += n


def _proj_prompt(x, gamma, w_in, b_f, tm=256):
    T, D = x.shape
    sizes = (512, 512, 512, 256, 32, 8, 512, 512, 512, 8, D, D)
    cuts = np.cumsum((0,) + sizes)
    col = lambda k: w_in[:, cuts[k]:cuts[k + 1]]
    q_a, k_a, v_a, q_i, k_i, w_i, q_b, k_b, v_b, f_b, g_a, g_b = [col(k) for k in range(12)]
    scale = HEAD_DIM ** -0.5
    idx_scale = (IDX_DIM ** -0.5) * (IDX_HEADS ** -0.5)
    k_i_pad = jnp.pad(k_i, ((0, 0), (0, LANES - IDX_DIM)))
    wn = jnp.concatenate([k_a, v_a, k_b, v_b, g_a, g_b, k_i_pad], axis=1).astype(_BF)
    pad8 = lambda w: jnp.pad(w, ((0, 0), (0, 8)))
    wt = jnp.concatenate([q_a * scale, q_b * scale, v_a, v_b, q_i, pad8(w_i * idx_scale), pad8(f_b)],
                         axis=1).T.astype(_BF)
    nn, nt = wn.shape[1], wt.shape[0]
    grid = (T // tm,)
    row = lambda n, dt=_F32: (jax.ShapeDtypeStruct((T, n), dt), pl.BlockSpec((tm, n), lambda i: (i, 0)))
    colT = lambda n, dt: (jax.ShapeDtypeStruct((n, T), dt), pl.BlockSpec((n, tm), lambda i: (0, i)))
    outs = [row(512), row(512), row(512), row(512), row(D), row(D), row(LANES), row(512, _BF),
            colT(512, _BF), colT(512, _BF), colT(512, _BF), colT(512, _BF), colT(256, _BF),
            colT(8, _F32), colT(8, _F32)]
    const = lambda shape: pl.BlockSpec(shape, lambda i: (0,) * len(shape), pipeline_mode=pl.Buffered(1))
    res = pl.pallas_call(
        _proj_prompt_kernel,
        grid=grid,
        in_specs=[pl.BlockSpec((tm, D), lambda i: (i, 0)), const((1, D)), const((D, nn)), const((nt, D)),
                  const((8, 1))],
        out_specs=[o[1] for o in outs],
        out_shape=[o[0] for o in outs],
        compiler_params=_cparams(("arbitrary",)),
        name="proj_prompt",
    )(x, gamma.reshape(1, D), wn, wt, b_f.reshape(8, 1))
    names = ("k_a", "v_a", "k_b", "v_b", "g_a", "g_b", "k_i", "k_a_bf",
             "q_aT", "q_bT", "v_aT", "v_bT", "q_iT", "w_iT", "logfT")
    return dict(zip(names, res))


def _cumsum_kernel(x_ref, lower_ref, o_ref):
    x = x_ref[...]
    n = x.shape[0]
    r = lax.broadcasted_iota(jnp.int32, (LANES, LANES), 0)
    c = lax.broadcasted_iota(jnp.int32, (LANES, LANES), 1)
    upper = (r <= c).astype(_BF)
    within = _dot3(x, upper)
    totals = within[:, LANES - 1:LANES]
    hi, mid, lo = _split3(totals)
    d = lambda a: jnp.dot(lower_ref[...], jnp.broadcast_to(a, (n, LANES)), preferred_element_type=_F32)
    o_ref[...] = within + (d(hi) + d(mid) + d(lo))


def _cumsum_time(logfT):
    H, T = logfT.shape
    C = T // LANES
    x = logfT.reshape(H * C, LANES)
    rows = np.arange(H * C)
    lower = (rows[:, None] // C == rows[None, :] // C) & (rows[None, :] < rows[:, None])
    out = pl.pallas_call(
        _cumsum_kernel,
        out_shape=jax.ShapeDtypeStruct((H * C, LANES), _F32),
        compiler_params=pltpu.CompilerParams(vmem_limit_bytes=VMEM_LIMIT),
        name="logf_cumsum",
    )(x, jnp.asarray(lower, _BF))
    return out.reshape(H, T)


def _causal_steps(nq, tq, tk):
    ii, jj = [], []
    for i in range(nq):
        for j in range(((i + 1) * tq - 1) // tk + 1):
            ii.append(i)
            jj.append(j)
    return np.asarray(ii, np.int32), np.asarray(jj, np.int32)


def _flash_update(h, s, vT, m_ref, l_ref, acc_ref):
    m_old = m_ref[h:h + 1, :]
    m_new = jnp.maximum(m_old, jnp.max(s, axis=0, keepdims=True))
    alpha = jnp.exp(m_old - m_new)
    p = jnp.exp(s - m_new)
    l_ref[h:h + 1, :] = alpha * l_ref[h:h + 1, :] + jnp.sum(p, axis=0, keepdims=True)
    pv = jnp.dot(vT, p.astype(_BF), preferred_element_type=_F32)
    rows = slice(h * HEAD_DIM, (h + 1) * HEAD_DIM)
    acc_ref[rows, :] = alpha * acc_ref[rows, :] + pv
    m_ref[h:h + 1, :] = m_new


def _flash_init(m_ref, l_ref, acc_ref):
    m_ref[...] = jnp.full(m_ref.shape, NEG, _F32)
    l_ref[...] = jnp.zeros(l_ref.shape, _F32)
    acc_ref[...] = jnp.zeros(acc_ref.shape, _F32)


def _flash_finish(o_ref, l_ref, acc_ref):
    for h in range(N_HEADS):
        rows = slice(h * HEAD_DIM, (h + 1) * HEAD_DIM)
        o_ref[rows, :] = (acc_ref[rows, :] / l_ref[h:h + 1, :]).astype(o_ref.dtype)


def _fox_kernel(it_ref, jt_ref, k_ref, vT_ref, qT_ref, o_ref, m_ref, l_ref, acc_ref, *, tq, tk):
    step = pl.program_id(0)
    i = it_ref[step]
    j = jt_ref[step]
    last_j = ((i + 1) * tq - 1) // tk

    @pl.when(j == 0)
    def _():
        _flash_init(m_ref, l_ref, acc_ref)

    def tile(masked):
        if masked:
            srow = j * tk + lax.broadcasted_iota(jnp.int32, (tk, tq), 0)
            tcol = i * tq + lax.broadcasted_iota(jnp.int32, (tk, tq), 1)
            keep = srow <= tcol
        for h in range(N_HEADS):
            s = jnp.dot(k_ref[h], qT_ref[h], preferred_element_type=_F32)
            if masked:
                s = jnp.where(keep, s, NEG)
            _flash_update(h, s, vT_ref[h * HEAD_DIM:(h + 1) * HEAD_DIM, :], m_ref, l_ref, acc_ref)

    needs_mask = (j + 1) * tk - 1 > i * tq

    @pl.when(needs_mask)
    def _():
        tile(True)

    @pl.when(jnp.logical_not(needs_mask))
    def _():
        tile(False)

    @pl.when(j == last_j)
    def _():
        _flash_finish(o_ref, l_ref, acc_ref)


def _fox_prompt(k_aug, vT, qT_aug, tq=256, tk=256):
    T = vT.shape[1]
    it, jt = _causal_steps(T // tq, tq, tk)
    gs = pltpu.PrefetchScalarGridSpec(
        num_scalar_prefetch=2,
        grid=(len(it),),
        in_specs=[pl.BlockSpec((N_HEADS, tk, LANES), lambda s, it, jt: (0, jt[s], 0)),
                  pl.BlockSpec((N_HEADS * HEAD_DIM, tk), lambda s, it, jt: (0, jt[s])),
                  pl.BlockSpec((N_HEADS, LANES, tq), lambda s, it, jt: (0, 0, it[s]))],
        out_specs=pl.BlockSpec((N_HEADS * HEAD_DIM, tq), lambda s, it, jt: (0, it[s])),
        scratch_shapes=[pltpu.VMEM((N_HEADS, tq), _F32), pltpu.VMEM((N_HEADS, tq), _F32),
                        pltpu.VMEM((N_HEADS * HEAD_DIM, tq), _F32)],
    )
    return pl.pallas_call(
        functools.partial(_fox_kernel, tq=tq, tk=tk),
        grid_spec=gs,
        out_shape=jax.ShapeDtypeStruct((N_HEADS * HEAD_DIM, T), _BF),
        compiler_params=_cparams(("arbitrary",)),
        name="fox_prompt",
    )(jnp.asarray(it), jnp.asarray(jt), k_aug, vT, qT_aug)


def _sort_key(x):
    b = lax.bitcast_convert_type(x, jnp.int32)
    return b ^ ((b >> 31) & jnp.int32(0x7FFFFFFF))


def _dsa_kernel(it_ref, jt_ref, ph_ref, ki_ref, qiT_ref, wiT_ref, k_ref, vT_ref, qT_ref, toep_ref,
                o_ref, key_ref, thr_ref, m_ref, l_ref, acc_ref, *, tq, topk):
    tk = tq
    step = pl.program_id(0)
    i = it_ref[step]
    j = jt_ref[step]
    phase = ph_ref[step]

    @pl.when(phase == 0)
    def _():
        w = wiT_ref[...]

        def chunk(c, carry):
            kc = ki_ref[pl.ds(pl.multiple_of(c * tk, tk), tk), :]
            acc = jnp.zeros((tk, tq), _F32)
            for h in range(IDX_HEADS):
                s = jnp.dot(kc, qiT_ref[h * IDX_DIM:(h + 1) * IDX_DIM, :], preferred_element_type=_F32)
                acc = acc + jnp.maximum(s, 0.0) * w[h:h + 1, :]
            key_ref[pl.ds(pl.multiple_of(c * tk, tk), tk), :] = _sort_key(acc)
            return carry

        lax.fori_loop(0, i, chunk, 0)
        chunk(i, 0)
        srow = lax.broadcasted_iota(jnp.int32, (tk, tq), 0)
        tcol = lax.broadcasted_iota(jnp.int32, (tk, tq), 1)
        diag = pl.ds(pl.multiple_of(i * tk, tk), tk)
        key_ref[diag, :] = jnp.where(srow <= tcol, key_ref[diag, :], INT_MIN)

        def bit_step(b, prefix):
            bit = 31 - b
            trial_u = prefix | lax.shift_left(jnp.int32(1), bit)
            trial_s = trial_u ^ jnp.int32(INT_MIN)

            def count(c, cnt):
                blk = key_ref[pl.ds(pl.multiple_of(c * tk, tk), tk), :]
                return cnt + jnp.sum(jnp.where(blk >= trial_s, 1, 0), axis=0, keepdims=True)

            cnt = lax.fori_loop(0, i + 1, count, jnp.zeros((1, tq), jnp.int32))
            return jnp.where(cnt >= topk, trial_u, prefix)

        prefix = lax.fori_loop(0, 32, bit_step, jnp.zeros((1, tq), jnp.int32))
        thr = prefix ^ jnp.int32(INT_MIN)
        thr_ref[...] = jnp.maximum(thr, INT_MIN + 1)
        _flash_init(m_ref, l_ref, acc_ref)

    @pl.when(phase == 1)
    def _():
        sel = key_ref[pl.ds(pl.multiple_of(j * tk, tk), tk), :] >= thr_ref[...]
        diff = (lax.broadcasted_iota(jnp.int32, (tk, tq), 1)
                - lax.broadcasted_iota(jnp.int32, (tk, tq), 0))
        near = j >= i - 1
        lo = jnp.where(j == i, 0, -tq)
        bias_on = (diff >= lo) & (diff < lo + tq)

        def tile(with_bias):
            for h in range(N_HEADS):
                p2 = h // 2
                s = jnp.dot(k_ref[:, p2 * LANES:(p2 + 1) * LANES], qT_ref[h], preferred_element_type=_F32)
                if with_bias:
                    s = s + jnp.where(bias_on, toep_ref[h], 0.0)
                s = jnp.where(sel, s, NEG)
                _flash_update(h, s, vT_ref[h * HEAD_DIM:(h + 1) * HEAD_DIM, :], m_ref, l_ref, acc_ref)

        @pl.when(near)
        def _():
            tile(True)

        @pl.when(jnp.logical_not(near))
        def _():
            tile(False)

        @pl.when(j == i)
        def _():
            _flash_finish(o_ref, l_ref, acc_ref)


def _bucket_table(n):
    max_exact = N_BUCKETS // 2
    d = np.arange(n)
    df = np.maximum(d.astype(np.float32), np.float32(1.0))
    large = max_exact + (np.log(df / max_exact) / math.log(MAX_DIST / max_exact)
                         * (N_BUCKETS - max_exact)).astype(np.int32)
    return np.where(d < max_exact, d, np.minimum(large, N_BUCKETS - 1)).astype(np.int32)


def _bucket_saturation():
    tab = _bucket_table(4 * MAX_DIST)
    assert tab[-1] == N_BUCKETS - 1 and np.all(np.diff(tab) >= 0)
    return int(np.argmax(tab == N_BUCKETS - 1))


def _dsa_prompt(k_i, q_iT, w_iT, k_bf, vT, qT_pad, rel_bias, tq=256):
    T = vT.shape[1]
    nq = T // tq
    topk = min(TOPK_MAX, T // 4)
    assert _bucket_saturation() <= tq
    tab = _bucket_table(tq)
    b = rel_bias[tab, :] - rel_bias[N_BUCKETS - 1][None, :]
    dmat = (np.arange(tq)[None, :] - np.arange(tq)[:, None]) % tq
    toep = jnp.transpose(b[dmat], (2, 0, 1)).astype(_F32)
    it, jt, ph = [], [], []
    for i in range(nq):
        it.append(i); jt.append(0); ph.append(0)
        for j in range(i + 1):
            it.append(i); jt.append(j); ph.append(1)
    it, jt, ph = (np.asarray(a, np.int32) for a in (it, jt, ph))
    const = lambda shape: pl.BlockSpec(shape, lambda s, it, jt, ph: (0,) * len(shape),
                                       pipeline_mode=pl.Buffered(1))
    gs = pltpu.PrefetchScalarGridSpec(
        num_scalar_prefetch=3,
        grid=(len(it),),
        in_specs=[const((T, IDX_DIM)),
                  pl.BlockSpec((IDX_HEADS * IDX_DIM, tq), lambda s, it, jt, ph: (0, it[s])),
                  pl.BlockSpec((IDX_HEADS, tq), lambda s, it, jt, ph: (0, it[s])),
                  pl.BlockSpec((tq, N_HEADS * HEAD_DIM), lambda s, it, jt, ph: (jt[s], 0)),
                  pl.BlockSpec((N_HEADS * HEAD_DIM, tq), lambda s, it, jt, ph: (0, jt[s])),
                  pl.BlockSpec((N_HEADS, LANES, tq), lambda s, it, jt, ph: (0, 0, it[s])),
                  const((N_HEADS, tq, tq))],
        out_specs=pl.BlockSpec((N_HEADS * HEAD_DIM, tq), lambda s, it, jt, ph: (0, it[s])),
        scratch_shapes=[pltpu.VMEM((T, tq), jnp.int32), pltpu.VMEM((1, tq), jnp.int32),
                        pltpu.VMEM((N_HEADS, tq), _F32), pltpu.VMEM((N_HEADS, tq), _F32),
                        pltpu.VMEM((N_HEADS * HEAD_DIM, tq), _F32)],
    )
    return pl.pallas_call(
        functools.partial(_dsa_kernel, tq=tq, topk=topk),
        grid_spec=gs,
        out_shape=jax.ShapeDtypeStruct((N_HEADS * HEAD_DIM, T), _BF),
        compiler_params=_cparams(("arbitrary",)),
        name="dsa_prompt",
    )(jnp.asarray(it), jnp.asarray(jt), jnp.asarray(ph), k_i, q_iT, w_iT, k_bf, vT, qT_pad, toep)


def _merge_kernel(oa_ref, ob_ref, ga_ref, gb_ref, x_ref, wa_ref, wb_ref, wo_ref, o_ref):
    ua = jnp.dot(oa_ref[...], wa_ref[...], preferred_element_type=_F32)
    ub = jnp.dot(ob_ref[...], wb_ref[...], preferred_element_type=_F32)
    merged = jax.nn.sigmoid(ga_ref[...]) * ua + jax.nn.sigmoid(gb_ref[...]) * ub
    o_ref[...] = x_ref[...] + jnp.dot(merged.astype(_BF), wo_ref[...], preferred_element_type=_F32)


def _merge(oa, ob, ga, gb, x, wa, wb, wo, tm):
    M, D = x.shape
    W = oa.shape[1]
    rowD = pl.BlockSpec((tm, D), lambda i: (i, 0))
    rowW = pl.BlockSpec((tm, W), lambda i: (i, 0))
    const = lambda shape: pl.BlockSpec(shape, lambda i: (0, 0), pipeline_mode=pl.Buffered(1))
    return pl.pallas_call(
        _merge_kernel,
        grid=(M // tm,),
        in_specs=[rowW, rowW, rowD, rowD, rowD, const((W, D)), const((W, D)), const((D, D))],
        out_specs=rowD,
        out_shape=jax.ShapeDtypeStruct((M, D), _F32),
        compiler_params=_cparams(("arbitrary",)),
        name="merge",
    )(oa, ob, ga, gb, x, wa.astype(_BF), wb.astype(_BF), wo.astype(_BF))


def _ffn_gate_down(y, x, wd_ref, dff):
    a = y[:, :dff]
    g = y[:, dff:]
    act = (g * jax.nn.sigmoid(g)) * a
    return x + jnp.dot(act.astype(_BF), wd_ref[...], preferred_element_type=_F32)


def _ffn_prompt_kernel(x_ref, g_ref, wu_ref, cw_ref, cb_ref, wd_ref, o_ref, st_ref, u_ref, *, tm, dff):
    @pl.when(pl.program_id(0) == 0)
    def _():
        u_ref[0:8, :] = jnp.zeros((8, 2 * dff), _F32)

    x = x_ref[...]
    h = _rms(x, g_ref[...]).astype(_BF)
    u_ref[8:8 + tm, :] = jnp.dot(h, wu_ref[...], preferred_element_type=_F32)
    y = (cb_ref[...] + cw_ref[0:1, :] * u_ref[6:6 + tm, :] + cw_ref[1:2, :] * u_ref[7:7 + tm, :]
         + cw_ref[2:3, :] * u_ref[8:8 + tm, :])
    o_ref[...] = _ffn_gate_down(y, x, wd_ref, dff)
    tail = u_ref[8 + tm - 2:8 + tm, :]
    st_ref[...] = tail
    u_ref[6:8, :] = tail


def _ffn_prompt(x, gamma, w_up, conv_w, conv_b, w_down, tm=256):
    T, D = x.shape
    dff = w_down.shape[0]
    const = lambda shape: pl.BlockSpec(shape, lambda i: (0, 0), pipeline_mode=pl.Buffered(1))
    return pl.pallas_call(
        functools.partial(_ffn_prompt_kernel, tm=tm, dff=dff),
        grid=(T // tm,),
        in_specs=[pl.BlockSpec((tm, D), lambda i: (i, 0)), const((1, D)), const((D, 2 * dff)),
                  const((3, 2 * dff)), const((1, 2 * dff)), const((dff, D))],
        out_specs=[pl.BlockSpec((tm, D), lambda i: (i, 0)), pl.BlockSpec((2, 2 * dff), lambda i: (0, 0))],
        out_shape=[jax.ShapeDtypeStruct((T, D), _F32), jax.ShapeDtypeStruct((2, 2 * dff), _F32)],
        scratch_shapes=[pltpu.VMEM((tm + 8, 2 * dff), _F32)],
        compiler_params=_cparams(("arbitrary",)),
        name="ffn_prompt",
    )(x, gamma.reshape(1, D), w_up.astype(_BF), conv_w, conv_b.reshape(1, -1), w_down.astype(_BF))


def _ffn_sample_kernel(x_ref, g_ref, wu_ref, cw_ref, cb_ref, wd_ref, s0_ref, s1_ref, o_ref, u_out_ref, *, dff):
    x = x_ref[...]
    h = _rms(x, g_ref[...]).astype(_BF)
    u = jnp.dot(h, wu_ref[...], preferred_element_type=_F32)
    y = cb_ref[...] + cw_ref[0:1, :] * s0_ref[...] + cw_ref[1:2, :] * s1_ref[...] + cw_ref[2:3, :] * u
    o_ref[...] = _ffn_gate_down(y, x, wd_ref, dff)
    u_out_ref[...] = u


def _ffn_sample(x, gamma, w_up, conv_w, conv_b, w_down, state):
    B, D = x.shape
    dff = w_down.shape[0]
    return pl.pallas_call(
        functools.partial(_ffn_sample_kernel, dff=dff),
        out_shape=[jax.ShapeDtypeStruct((B, D), _F32), jax.ShapeDtypeStruct((B, 2 * dff), _F32)],
        compiler_params=pltpu.CompilerParams(vmem_limit_bytes=VMEM_LIMIT),
        name="ffn_sample",
    )(x, gamma.reshape(1, D), w_up.astype(_BF), conv_w, conv_b.reshape(1, -1), w_down.astype(_BF),
      state[:, 0, :], state[:, 1, :])


def _norm_kernel(x_ref, g_ref, o_ref):
    o_ref[...] = _rms(x_ref[...], g_ref[...])


def _final_norm(x, gamma, tm):
    M, D = x.shape
    return pl.pallas_call(
        _norm_kernel,
        grid=(M // tm,),
        in_specs=[pl.BlockSpec((tm, D), lambda i: (i, 0)), pl.BlockSpec((1, D), lambda i: (0, 0))],
        out_specs=pl.BlockSpec((tm, D), lambda i: (i, 0)),
        out_shape=jax.ShapeDtypeStruct((M, D), _F32),
        compiler_params=_cparams(("arbitrary",)),
        name="final_norm",
    )(x, gamma.reshape(1, D))


def _proj_sample_kernel(x_ref, g_ref, w_ref, o_ref):
    h = _rms(x_ref[...], g_ref[...]).astype(_BF)
    o_ref[...] = jnp.dot(h, w_ref[...], preferred_element_type=_F32)


def _proj_sample(x, gamma, w_in):
    B, D = x.shape
    n = w_in.shape[1]
    npad = -(-n // LANES) * LANES
    w = jnp.pad(w_in, ((0, 0), (0, npad - n))).astype(_BF)
    z = pl.pallas_call(
        _proj_sample_kernel,
        out_shape=jax.ShapeDtypeStruct((B, npad), _F32),
        compiler_params=pltpu.CompilerParams(vmem_limit_bytes=VMEM_LIMIT),
        name="proj_sample",
    )(x, gamma.reshape(1, D), w)
    return z[:, :n]


def _sample_scores_kernel(pt_ref, page_ref, q_ref, w_ref, knew_ref, o_ref, onew_ref):
    q = q_ref[0]
    w = w_ref[0]
    s = _dot_nt(q, page_ref[0].astype(_BF))
    o_ref[0] = jnp.sum(jnp.maximum(s, 0.0) * w, axis=0, keepdims=True)

    @pl.when(pl.program_id(1) == 0)
    def _():
        sn = _dot_nt(q, knew_ref[0].astype(_BF))
        onew_ref[0] = jnp.sum(jnp.maximum(sn, 0.0) * w, axis=0, keepdims=True)


def _sample_scores(cache_kidx, layer, page_table, q_idx, w_idx, k_new):
    depth, n_pool = cache_kidx.shape[:2]
    B, NP = page_table.shape
    pool = cache_kidx.reshape(depth * n_pool, PAGE_SIZE, IDX_DIM)
    base = layer * n_pool
    idx_scale = (IDX_DIM ** -0.5) * (IDX_HEADS ** -0.5)
    knew8 = jnp.pad(k_new[:, None, :], ((0, 0), (0, 7), (0, 0)))
    gs = pltpu.PrefetchScalarGridSpec(
        num_scalar_prefetch=1,
        grid=(B, NP),
        in_specs=[pl.BlockSpec((1, PAGE_SIZE, IDX_DIM), lambda b, p, pt: (base + pt[b, p], 0, 0)),
                  pl.BlockSpec((1, IDX_HEADS, IDX_DIM), lambda b, p, pt: (b, 0, 0)),
                  pl.BlockSpec((1, IDX_HEADS, 1), lambda b, p, pt: (b, 0, 0)),
                  pl.BlockSpec((1, 8, IDX_DIM), lambda b, p, pt: (b, 0, 0))],
        out_specs=[pl.BlockSpec((1, 1, PAGE_SIZE), lambda b, p, pt: (b * NP + p, 0, 0)),
                   pl.BlockSpec((1, 1, 8), lambda b, p, pt: (b, 0, 0))],
    )
    sc, sn = pl.pallas_call(
        _sample_scores_kernel,
        grid_spec=gs,
        out_shape=[jax.ShapeDtypeStruct((B * NP, 1, PAGE_SIZE), _F32), jax.ShapeDtypeStruct((B, 1, 8), _F32)],
        compiler_params=_cparams(("arbitrary", "arbitrary")),
        name="sample_scores",
    )(page_table, pool, q_idx.astype(_BF), (w_idx * idx_scale)[:, :, None], knew8)
    return jnp.concatenate([sc.reshape(B, NP * PAGE_SIZE), sn[:, 0, :1]], axis=1)


def _topk_kernel(x_ref, o_ref, buf_ref, *, k):
    B, L = x_ref.shape
    buf_ref[...] = x_ref[...]
    lane = lax.broadcasted_iota(jnp.int32, (B, L), 1).astype(_F32)
    col = lax.broadcasted_iota(jnp.int32, (B, k), 1)

    def body(r, out):
        x = buf_ref[...]
        m = jnp.max(x, axis=1, keepdims=True)
        idx = jnp.min(jnp.where(x == m, lane, float(L)), axis=1, keepdims=True)
        buf_ref[...] = jnp.where(lane == idx, -jnp.inf, x)
        return jnp.where(col == r, idx, out)

    o_ref[...] = lax.fori_loop(0, k, body, jnp.zeros((B, k), _F32)).astype(jnp.int32)


def _topk_indices(scores, k):
    B, n = scores.shape
    L = -(-n // LANES) * LANES
    x = jnp.pad(scores, ((0, 0), (0, L - n)), constant_values=-jnp.inf)
    return pl.pallas_call(
        functools.partial(_topk_kernel, k=k),
        out_shape=jax.ShapeDtypeStruct((B, k), jnp.int32),
        scratch_shapes=[pltpu.VMEM((B, L), _F32)],
        compiler_params=pltpu.CompilerParams(vmem_limit_bytes=VMEM_LIMIT),
        name="sample_topk",
    )(x)


def _head_block_mask():
    r = lax.broadcasted_iota(jnp.int32, (N_HEADS, N_HEADS * HEAD_DIM), 0)
    c = lax.broadcasted_iota(jnp.int32, (N_HEADS, N_HEADS * HEAD_DIM), 1)
    return (c // HEAD_DIM) == r


def _bucket_edges():
    tab = _bucket_table(4 * MAX_DIST)
    max_exact = N_BUCKETS // 2
    edges = [int(np.argmax(tab >= b)) for b in range(max_exact + 1, N_BUCKETS)]
    for b, e in zip(range(max_exact + 1, N_BUCKETS), edges):
        assert tab[e] == b and tab[e - 1] == b - 1
    return edges


def _dsa_sample_kernel(idx_sm, pt_sm, kpool_ref, vpool_ref, knew_ref, vnew_ref, idx_ref, qbd_ref, rbT_ref,
                       o_ref, kbuf, vbuf, sem, *, k, past, base_row):
    b = pl.program_id(0)

    def pool_row(r):
        sp = jnp.minimum(idx_sm[b, r], past - 1)
        page = lax.shift_right_logical(sp, int(math.log2(PAGE_SIZE)))
        return base_row + pt_sm[b, page] * PAGE_SIZE + (sp & (PAGE_SIZE - 1))

    def copies(k_src, v_src, row, r):
        dst = pl.ds(r, 1)
        return (pltpu.make_async_copy(k_src.at[pl.ds(row, 1), :], kbuf.at[dst, :], sem.at[0]),
                pltpu.make_async_copy(v_src.at[pl.ds(row, 1), :], vbuf.at[dst, :], sem.at[1]))

    def start(r, c):
        is_new = idx_sm[b, r] >= past

        @pl.when(is_new)
        def _():
            for cp in copies(knew_ref, vnew_ref, b, r):
                cp.start()

        @pl.when(jnp.logical_not(is_new))
        def _():
            for cp in copies(kpool_ref, vpool_ref, pool_row(r), r):
                cp.start()
        return c

    lax.fori_loop(0, k, start, 0)

    def wait(r, c):
        for cp in copies(kpool_ref, vpool_ref, pool_row(r), r):
            cp.wait()
        return c

    lax.fori_loop(0, k, wait, 0)

    idx = idx_ref[0]
    dist = past - idx
    max_exact = N_BUCKETS // 2
    bucket = jnp.full(idx.shape, max_exact, jnp.int32)
    for e in _bucket_edges():
        bucket = bucket + jnp.where(dist >= e, 1, 0)
    bucket = jnp.where(dist < max_exact, dist, bucket)
    onehot = (lax.broadcasted_iota(jnp.int32, (N_BUCKETS, k), 0) == bucket).astype(_BF)
    bias = _dot3(rbT_ref[...], onehot)
    logits = _dot_nt(qbd_ref[0], kbuf[...].astype(_BF)) + bias
    m = jnp.max(logits, axis=1, keepdims=True)
    p = jnp.exp(logits - m)
    denom = jnp.sum(p, axis=1, keepdims=True)
    full = jnp.dot(p.astype(_BF), vbuf[...].astype(_BF), preferred_element_type=_F32) / denom
    o_ref[0] = jnp.sum(jnp.where(_head_block_mask(), full, 0.0), axis=0, keepdims=True)


def _dsa_sample(cache_k, cache_v, layer, page_table, idx, q, k_new, v_new, rel_bias):
    depth, n_pool = cache_k.shape[:2]
    B, NP = page_table.shape
    K = idx.shape[1]
    W = N_HEADS * HEAD_DIM
    kpool = cache_k.reshape(depth * n_pool * PAGE_SIZE, W)
    vpool = cache_v.reshape(depth * n_pool * PAGE_SIZE, W)
    qbd = _block_diag_q(q)
    any_spec = pl.BlockSpec(memory_space=pl.ANY)
    gs = pltpu.PrefetchScalarGridSpec(
        num_scalar_prefetch=2,
        grid=(B,),
        in_specs=[any_spec, any_spec, any_spec, any_spec,
                  pl.BlockSpec((1, 1, K), lambda b, i, p: (b, 0, 0)),
                  pl.BlockSpec((1, N_HEADS, W), lambda b, i, p: (b, 0, 0)),
                  pl.BlockSpec((N_HEADS, N_BUCKETS), lambda b, i, p: (0, 0))],
        out_specs=pl.BlockSpec((1, 1, W), lambda b, i, p: (b, 0, 0)),
        scratch_shapes=[pltpu.VMEM((K, W), _F32), pltpu.VMEM((K, W), _F32), pltpu.SemaphoreType.DMA((2,))],
    )
    out = pl.pallas_call(
        functools.partial(_dsa_sample_kernel, k=K, past=NP * PAGE_SIZE, base_row=layer * n_pool * PAGE_SIZE),
        grid_spec=gs,
        out_shape=jax.ShapeDtypeStruct((B, 1, W), _F32),
        compiler_params=_cparams(("arbitrary",)),
        name="dsa_sample",
    )(idx, page_table, kpool, vpool, k_new, v_new, idx[:, None, :], qbd, rel_bias.T)
    return out[:, 0, :]


def _block_diag_q(q):
    B = q.shape[0]
    qh = (q * (HEAD_DIM ** -0.5)).reshape(B, N_HEADS, 1, HEAD_DIM)
    eye = jnp.eye(N_HEADS, dtype=q.dtype)[None, :, :, None]
    return (qh * eye).reshape(B, N_HEADS, N_HEADS * HEAD_DIM).astype(_BF)


def _fox_sample_kernel(pt_ref, kpage_ref, vpage_ref, lf_ref, qbd_ref, knew_ref, vnew_ref, lfnew_ref,
                       o_ref, m_ref, l_ref, acc_ref, c_ref, *, n_pages):
    p = pl.program_id(1)

    @pl.when(p == 0)
    def _():
        m_ref[...] = jnp.full(m_ref.shape, NEG, _F32)
        l_ref[...] = jnp.zeros(l_ref.shape, _F32)
        acc_ref[...] = jnp.zeros(acc_ref.shape, _F32)
        c_ref[...] = jnp.zeros(c_ref.shape, _F32)

    qbd = qbd_ref[0]
    r = lax.broadcasted_iota(jnp.int32, (LANES, LANES), 0)
    c = lax.broadcasted_iota(jnp.int32, (LANES, LANES), 1)
    upper = (r <= c).astype(_BF)
    csum = c_ref[...] + _dot3(lf_ref[0], upper)
    s = _dot_nt(qbd, kpage_ref[0].astype(_BF)) - csum
    m_old = m_ref[...]
    m_new = jnp.maximum(m_old, jnp.max(s, axis=1, keepdims=True))
    alpha = jnp.exp(m_old - m_new)
    pr = jnp.exp(s - m_new)
    l_ref[...] = alpha * l_ref[...] + jnp.sum(pr, axis=1, keepdims=True)
    acc_ref[...] = alpha * acc_ref[...] + jnp.dot(pr.astype(_BF), vpage_ref[0].astype(_BF),
                                                  preferred_element_type=_F32)
    m_ref[...] = m_new
    c_ref[...] = csum[:, LANES - 1:LANES]

    @pl.when(p == n_pages - 1)
    def _():
        c_new = c_ref[...] + lfnew_ref[0]
        s_new = jnp.sum(qbd.astype(_F32) * knew_ref[0], axis=1, keepdims=True) - c_new
        m_o = m_ref[...]
        m_n = jnp.maximum(m_o, s_new)
        a = jnp.exp(m_o - m_n)
        p_new = jnp.exp(s_new - m_n)
        l_fin = a * l_ref[...] + p_new
        acc = a * acc_ref[...] + p_new * vnew_ref[0]
        o_ref[0] = jnp.sum(jnp.where(_head_block_mask(), acc / l_fin, 0.0), axis=0, keepdims=True)


def _fox_sample(cache_k, cache_v, logf_poolT, layer, page_table, q, k_new, v_new, logf_new):
    depth, n_pool = cache_k.shape[:2]
    B, NP = page_table.shape
    W = N_HEADS * HEAD_DIM
    kpool = cache_k.reshape(depth * n_pool, PAGE_SIZE, W)
    vpool = cache_v.reshape(depth * n_pool, PAGE_SIZE, W)
    base = layer * n_pool
    page = lambda b, p, pt: (base + pt[b, p], 0, 0)
    per_b = lambda b, p, pt: (b, 0, 0)
    gs = pltpu.PrefetchScalarGridSpec(
        num_scalar_prefetch=1,
        grid=(B, NP),
        in_specs=[pl.BlockSpec((1, PAGE_SIZE, W), page), pl.BlockSpec((1, PAGE_SIZE, W), page),
                  pl.BlockSpec((1, N_HEADS, PAGE_SIZE), page),
                  pl.BlockSpec((1, N_HEADS, W), per_b), pl.BlockSpec((1, 1, W), per_b),
                  pl.BlockSpec((1, 1, W), per_b), pl.BlockSpec((1, N_HEADS, 1), per_b)],
        out_specs=pl.BlockSpec((1, 1, W), per_b),
        scratch_shapes=[pltpu.VMEM((N_HEADS, 1), _F32), pltpu.VMEM((N_HEADS, 1), _F32),
                        pltpu.VMEM((N_HEADS, W), _F32), pltpu.VMEM((N_HEADS, 1), _F32)],
    )
    out = pl.pallas_call(
        functools.partial(_fox_sample_kernel, n_pages=NP),
        grid_spec=gs,
        out_shape=jax.ShapeDtypeStruct((B, 1, W), _F32),
        compiler_params=_cparams(("arbitrary", "arbitrary")),
        name="fox_sample",
    )(page_table, kpool, vpool, logf_poolT, _block_diag_q(q), k_new[:, None, :], v_new[:, None, :],
      logf_new[:, :, None])
    return out[:, 0, :]


def _prompt_attention_inputs(pp):
    T = pp["k_a"].shape[0]
    c = _cumsum_time(pp["logfT"])
    hi, mid, lo = _split3(-c)
    kb = pp["k_b"].reshape(T, N_HEADS, HEAD_DIM).transpose(1, 0, 2).astype(_BF)
    cols = jnp.stack([hi, mid, lo], axis=-1)
    k_aug = jnp.concatenate([kb, cols, jnp.zeros((N_HEADS, T, LANES - HEAD_DIM - 3), _BF)], axis=-1)
    qb = pp["q_bT"].reshape(N_HEADS, HEAD_DIM, T)
    ones = jnp.ones((N_HEADS, 3, T), _BF)
    q_aug = jnp.concatenate([qb, ones, jnp.zeros((N_HEADS, LANES - HEAD_DIM - 3, T), _BF)], axis=1)
    qa = pp["q_aT"].reshape(N_HEADS // 2, 2, HEAD_DIM, T)
    z = jnp.zeros_like(qa[:, 0])
    even = jnp.concatenate([qa[:, 0], z], axis=1)
    odd = jnp.concatenate([z, qa[:, 1]], axis=1)
    q_pad = jnp.stack([even, odd], axis=1).reshape(N_HEADS, LANES, T)
    return k_aug, q_aug, q_pad


def kernel(x_prompt, x_sample, cache_k_a, cache_v_a, cache_kidx, cache_k_b, cache_v_b, cache_logf_b,
           state_conv, page_table, rel_bias, attn_norm, w_in, b_forget, w_a_up, w_b_up, w_o,
           ffn_norm, w_up, conv_w, conv_b, w_down, final_norm):
    depth = w_in.shape[0]
    Bp, T, D = x_prompt.shape
    Bs = x_sample.shape[0]
    assert Bp == 1 and x_sample.shape[1] == 1
    n_pool = cache_k_a.shape[1]
    NP = page_table.shape[1]
    P = NP * PAGE_SIZE
    W = N_HEADS * HEAD_DIM
    topk_s = min(TOPK_MAX, (P + 1) // 4)
    tm = min(512, T)

    xp = x_prompt[0]
    xs = x_sample[:, 0, :]
    logf_poolT = jnp.swapaxes(cache_logf_b, 2, 3).reshape(depth * n_pool, N_HEADS, PAGE_SIZE)
    sizes = (W, W, W, IDX_HEADS * IDX_DIM, IDX_DIM, IDX_HEADS, W, W, W, N_HEADS, D, D)
    cuts = np.cumsum((0,) + sizes)
    new = [[] for _ in range(14)]
    for l in range(depth):
        pp = _proj_prompt(xp, attn_norm[l], w_in[l], b_forget[l])
        k_aug, q_aug, q_pad = _prompt_attention_inputs(pp)
        obT = _fox_prompt(k_aug, pp["v_bT"], q_aug)
        oaT = _dsa_prompt(pp["k_i"][:, :IDX_DIM].astype(_BF), pp["q_iT"], pp["w_iT"], pp["k_a_bf"],
                          pp["v_aT"], q_pad, rel_bias)
        xp = _merge(oaT.T, obT.T, pp["g_a"], pp["g_b"], xp, w_a_up[l], w_b_up[l], w_o[l], tm)
        xp, conv_p = _ffn_prompt(xp, ffn_norm[l], w_up[l], conv_w[l], conv_b[l], w_down[l])

        z = _proj_sample(xs, attn_norm[l], w_in[l])
        (q_a, k_a, v_a, q_i, k_i, w_i, q_b, k_b, v_b, f_b, g_a, g_b) = [z[:, cuts[k]:cuts[k + 1]] for k in range(12)]
        logf_s = _log_sigmoid_rows(f_b, b_forget[l])
        scores = _sample_scores(cache_kidx, l, page_table, q_i.reshape(Bs, IDX_HEADS, IDX_DIM), w_i, k_i)
        idx = _topk_indices(scores, topk_s)
        oa_s = _dsa_sample(cache_k_a, cache_v_a, l, page_table, idx, q_a, k_a, v_a, rel_bias)
        ob_s = _fox_sample(cache_k_b, cache_v_b, logf_poolT, l, page_table, q_b, k_b, v_b, logf_s)
        xs = _merge(oa_s.astype(_BF), ob_s.astype(_BF), g_a, g_b, xs, w_a_up[l], w_b_up[l], w_o[l], Bs)
        xs, u_s = _ffn_sample(xs, ffn_norm[l], w_up[l], conv_w[l], conv_b[l], w_down[l], state_conv[l])
        conv_s = jnp.stack([state_conv[l][:, 1, :], u_s], axis=1)

        hd = lambda t, n: t.reshape(t.shape[0], n, HEAD_DIM)
        rows = (hd(pp["k_a"], N_HEADS)[None], hd(k_a, N_HEADS)[:, None],
                hd(pp["v_a"], N_HEADS)[None], hd(v_a, N_HEADS)[:, None],
                pp["k_i"][None, :, :IDX_DIM], k_i[:, None],
                hd(pp["k_b"], N_HEADS)[None], hd(k_b, N_HEADS)[:, None],
                hd(pp["v_b"], N_HEADS)[None], hd(v_b, N_HEADS)[:, None],
                pp["logfT"].T[None], logf_s[:, None],
                conv_p[None], conv_s)
        for lst, r in zip(new, rows):
            lst.append(r)
    y_prompt = _final_norm(xp, final_norm, tm)[None]
    y_sample = _final_norm(xs, final_norm, Bs)[:, None]
    return (y_prompt, y_sample) + tuple(jnp.stack(n) for n in new)


def _logsig_kernel(f_ref, b_ref, o_ref):
    o_ref[...] = _log_sigmoid(f_ref[...] + b_ref[...])


def _log_sigmoid_rows(f, b):
    return pl.pallas_call(
        _logsig_kernel,
        out_shape=jax.ShapeDtypeStruct(f.shape, _F32),
        name="sample_logf",
    )(f, b.reshape(1, -1))
```

```python
import functools
import math

import numpy as np
import jax
import jax.numpy as jnp
from jax import lax
from jax.experimental import pallas as pl
from jax.experimental.pallas import tpu as pltpu

HEAD_DIM = 64
N_HEADS = 8
IDX_HEADS = 8
IDX_DIM = 32
TOPK_MAX = 256
N_BUCKETS = 32
MAX_DIST = 128
PAGE_SIZE = 128
NORM_EPS = 1e-6
NEG = -1e30
INT_MIN = -2 ** 31
LANES = 128
V_ROWS = 80
VMEM_LIMIT = 56 * 1024 * 1024
LOG2E = math.log2(math.e)

_BF = jnp.bfloat16
_F32 = jnp.float32


def _cparams(sem):
    return pltpu.CompilerParams(dimension_semantics=sem, vmem_limit_bytes=VMEM_LIMIT)


def _split3(x):
    hi = x.astype(_BF)
    r1 = x - hi.astype(_F32)
    mid = r1.astype(_BF)
    lo = (r1 - mid.astype(_F32)).astype(_BF)
    return hi, mid, lo


def _dot3(x, b):
    hi, mid, lo = _split3(x)
    d = lambda a: jnp.dot(a, b, preferred_element_type=_F32)
    return d(hi) + d(mid) + d(lo)


def _rms(x, g):
    var = jnp.mean(x * x, axis=-1, keepdims=True)
    return x * lax.rsqrt(var + NORM_EPS) * g


def _log_sigmoid(x):
    return -(jnp.maximum(-x, 0.0) + jnp.log1p(jnp.exp(-jnp.abs(x))))


def _dot_nt(a, b):
    return lax.dot_general(a, b, (((1,), (1,)), ((), ())), preferred_element_type=_F32)


def _bdot(a, b):
    return lax.dot_general(a, b, (((2,), (1,)), ((0,), (0,))), preferred_element_type=_F32)


_NAT = (("k_a", 512), ("v_a", 512), ("k_b", 512), ("v_b", 512), ("g_a", 1024), ("g_b", 1024), ("k_i", 128))
_TRN = (("q_a", 512), ("q_b", 512), ("v_a", 512), ("v_b", 512), ("q_i", 256), ("w_i", 16), ("f_b", 16))


def _proj_prompt_kernel(x_ref, g_ref, wn_ref, wt_ref, bf_ref,
                        ka_ref, va_ref, kb_ref, vb_ref, ga_ref, gb_ref, ki_ref,
                        qaT_ref, qbT_ref, vaT_ref, vbT_ref, qiT_ref, wiT_ref, lfT_ref):
    h = _rms(x_ref[...], g_ref[...]).astype(_BF)
    nat_refs = (ka_ref, va_ref, kb_ref, vb_ref, ga_ref, gb_ref, ki_ref)
    off = 0
    for (name, n), ref in zip(_NAT, nat_refs):
        ref[...] = jnp.dot(h, wn_ref[:, off:off + n], preferred_element_type=_F32)
        off += n
    trn_refs = (qaT_ref, qbT_ref, vaT_ref, vbT_ref, qiT_ref, wiT_ref, lfT_ref)
    off = 0
    for (name, n), ref in zip(_TRN, trn_refs):
        z = _dot_nt(wt_ref[off:off + n, :], h)[:ref.shape[0]]
        if name == "f_b":
            z = _log_sigmoid(z + bf_ref[...])
        ref[...] = z.astype(ref.dtype)
        off += n


def _proj_prompt(x, gamma, w_in, b_f, tm=256):
    T, D = x.shape
    sizes = (512, 512, 512, 256, 32, 8, 512, 512, 512, 8, D, D)
    cuts = np.cumsum((0,) + sizes)
    col = lambda k: w_in[:, cuts[k]:cuts[k + 1]]
    q_a, k_a, v_a, q_i, k_i, w_i, q_b, k_b, v_b, f_b, g_a, g_b = [col(k) for k in range(12)]
    scale = (HEAD_DIM ** -0.5) * LOG2E
    idx_scale = (IDX_DIM ** -0.5) * (IDX_HEADS ** -0.5)
    k_i_pad = jnp.pad(k_i, ((0, 0), (0, LANES - IDX_DIM)))
    wn = jnp.concatenate([k_a, v_a, k_b, v_b, g_a, g_b, k_i_pad], axis=1).astype(_BF)
    pad8 = lambda w: jnp.pad(w, ((0, 0), (0, 8)))
    wt = jnp.concatenate([q_a * scale, q_b * scale, v_a, v_b, q_i, pad8(w_i * idx_scale), pad8(f_b)],
                         axis=1).T.astype(_BF)
    nn, nt = wn.shape[1], wt.shape[0]
    grid = (T // tm,)
    row = lambda n, dt=_F32: (jax.ShapeDtypeStruct((T, n), dt), pl.BlockSpec((tm, n), lambda i: (i, 0)))
    colT = lambda n, dt: (jax.ShapeDtypeStruct((n, T), dt), pl.BlockSpec((n, tm), lambda i: (0, i)))
    outs = [row(512), row(512), row(512), row(512), row(D), row(D), row(LANES),
            colT(512, _BF), colT(512, _BF), colT(512, _BF), colT(512, _BF), colT(256, _BF),
            colT(8, _F32), colT(8, _F32)]
    const = lambda shape: pl.BlockSpec(shape, lambda i: (0,) * len(shape), pipeline_mode=pl.Buffered(1))
    res = pl.pallas_call(
        _proj_prompt_kernel,
        grid=grid,
        in_specs=[pl.BlockSpec((tm, D), lambda i: (i, 0)), const((1, D)), const((D, nn)), const((nt, D)),
                  const((8, 1))],
        out_specs=[o[1] for o in outs],
        out_shape=[o[0] for o in outs],
        compiler_params=_cparams(("arbitrary",)),
        name="proj_prompt",
    )(x, gamma.reshape(1, D), wn, wt, b_f.reshape(8, 1))
    names = ("k_a", "v_a", "k_b", "v_b", "g_a", "g_b", "k_i",
             "q_aT", "q_bT", "v_aT", "v_bT", "q_iT", "w_iT", "logfT")
    return dict(zip(names, res))


def _cumsum_kernel(x_ref, lower_ref, o_ref):
    x = x_ref[...]
    n = x.shape[0]
    r = lax.broadcasted_iota(jnp.int32, (LANES, LANES), 0)
    c = lax.broadcasted_iota(jnp.int32, (LANES, LANES), 1)
    upper = (r <= c).astype(_BF)
    within = _dot3(x, upper)
    totals = within[:, LANES - 1:LANES]
    hi, mid, lo = _split3(totals)
    d = lambda a: jnp.dot(lower_ref[...], jnp.broadcast_to(a, (n, LANES)), preferred_element_type=_F32)
    o_ref[...] = within + (d(hi) + d(mid) + d(lo))


def _cumsum_time(logfT):
    H, T = logfT.shape
    C = T // LANES
    x = logfT.reshape(H * C, LANES)
    rows = np.arange(H * C)
    lower = (rows[:, None] // C == rows[None, :] // C) & (rows[None, :] < rows[:, None])
    out = pl.pallas_call(
        _cumsum_kernel,
        out_shape=jax.ShapeDtypeStruct((H * C, LANES), _F32),
        compiler_params=pltpu.CompilerParams(vmem_limit_bytes=VMEM_LIMIT),
        name="logf_cumsum",
    )(x, jnp.asarray(lower, _BF))
    return out.reshape(H, T)


def _attn_init(m_ref, acc_ref):
    m_ref[...] = jnp.full(m_ref.shape, NEG, _F32)
    acc_ref[...] = jnp.zeros(acc_ref.shape, _F32)


def _attn_tile(k_ref, qT_ref, vT_ref, m_ref, acc_ref, keep=None, bias=None):
    s = _bdot(k_ref[...], qT_ref[...])
    if bias is not None:
        s = s + bias
    if keep is not None:
        s = jnp.where(keep[None], s, NEG)
    m_old = m_ref[:, 0:1, :]
    m_new = jnp.maximum(m_old, jnp.max(s, axis=1, keepdims=True))
    alpha = jnp.exp2(m_old - m_new)
    p = jnp.exp2(s - m_new).astype(_BF)
    acc_ref[...] = alpha * acc_ref[...] + _bdot(vT_ref[...], p)
    m_ref[...] = jnp.broadcast_to(m_new, m_ref.shape)


def _attn_finish(o_ref, acc_ref):
    out = acc_ref[:, :HEAD_DIM, :] / acc_ref[:, HEAD_DIM:HEAD_DIM + 1, :]
    o_ref[...] = out.reshape(o_ref.shape).astype(o_ref.dtype)


def _values_with_ones(vT):
    T = vT.shape[1]
    v = vT.reshape(N_HEADS, HEAD_DIM, T)
    ones = jnp.ones((N_HEADS, 1, T), _BF)
    zeros = jnp.zeros((N_HEADS, V_ROWS - HEAD_DIM - 1, T), _BF)
    return jnp.concatenate([v, ones, zeros], axis=1)


def _fox_kernel(it_ref, jt_ref, k_ref, vT_ref, qT_ref, o_ref, m_ref, acc_ref, *, tq, tk):
    step = pl.program_id(0)
    i = it_ref[step]
    j = jt_ref[step]
    last_j = ((i + 1) * tq - 1) // tk

    @pl.when(j == 0)
    def _():
        _attn_init(m_ref, acc_ref)

    needs_mask = (j + 1) * tk - 1 > i * tq

    @pl.when(needs_mask)
    def _():
        srow = j * tk + lax.broadcasted_iota(jnp.int32, (tk, tq), 0)
        tcol = i * tq + lax.broadcasted_iota(jnp.int32, (tk, tq), 1)
        _attn_tile(k_ref, qT_ref, vT_ref, m_ref, acc_ref, keep=srow <= tcol)

    @pl.when(jnp.logical_not(needs_mask))
    def _():
        _attn_tile(k_ref, qT_ref, vT_ref, m_ref, acc_ref)

    @pl.when(j == last_j)
    def _():
        _attn_finish(o_ref, acc_ref)


def _causal_steps(nq, tq, tk):
    ii, jj = [], []
    for i in range(nq):
        for j in range(((i + 1) * tq - 1) // tk + 1):
            ii.append(i)
            jj.append(j)
    return np.asarray(ii, np.int32), np.asarray(jj, np.int32)


def _fox_prompt(k_aug, vT_aug, qT_aug, tq=512, tk=512):
    T = k_aug.shape[1]
    tq, tk = min(tq, T), min(tk, T)
    it, jt = _causal_steps(T // tq, tq, tk)
    gs = pltpu.PrefetchScalarGridSpec(
        num_scalar_prefetch=2,
        grid=(len(it),),
        in_specs=[pl.BlockSpec((N_HEADS, tk, LANES), lambda s, it, jt: (0, jt[s], 0)),
                  pl.BlockSpec((N_HEADS, V_ROWS, tk), lambda s, it, jt: (0, 0, jt[s])),
                  pl.BlockSpec((N_HEADS, LANES, tq), lambda s, it, jt: (0, 0, it[s]))],
        out_specs=pl.BlockSpec((N_HEADS * HEAD_DIM, tq), lambda s, it, jt: (0, it[s])),
        scratch_shapes=[pltpu.VMEM((N_HEADS, 8, tq), _F32), pltpu.VMEM((N_HEADS, V_ROWS, tq), _F32)],
    )
    return pl.pallas_call(
        functools.partial(_fox_kernel, tq=tq, tk=tk),
        grid_spec=gs,
        out_shape=jax.ShapeDtypeStruct((N_HEADS * HEAD_DIM, T), _BF),
        compiler_params=_cparams(("arbitrary",)),
        name="fox_prompt",
    )(jnp.asarray(it), jnp.asarray(jt), k_aug, vT_aug, qT_aug)


def _sort_key(x):
    b = lax.bitcast_convert_type(x, jnp.int32)
    return b ^ ((b >> 31) & jnp.int32(0x7FFFFFFF))


def _dsa_kernel(it_ref, jt_ref, ph_ref, ki_ref, qiT_ref, wiT_ref, k_ref, vT_ref, qT_ref, toep_ref,
                o_ref, key_ref, thr_ref, m_ref, acc_ref, *, tq, tk, topk):
    step = pl.program_id(0)
    i = it_ref[step]
    j = jt_ref[step]
    phase = ph_ref[step]
    last_j = ((i + 1) * tq - 1) // tk
    chunk_rows = lambda c: pl.ds(pl.multiple_of(c * tq, tq), tq)

    @pl.when(phase == 0)
    def _():
        w = wiT_ref[...]

        def chunk(c, carry):
            kc = ki_ref[chunk_rows(c), :]
            acc = jnp.zeros((tq, tq), _F32)
            for h in range(IDX_HEADS):
                s = jnp.dot(kc, qiT_ref[h * IDX_DIM:(h + 1) * IDX_DIM, :], preferred_element_type=_F32)
                acc = acc + jnp.maximum(s, 0.0) * w[h:h + 1, :]
            key_ref[chunk_rows(c), :] = _sort_key(acc)
            return carry

        lax.fori_loop(0, i, chunk, 0)
        chunk(i, 0)
        srow = lax.broadcasted_iota(jnp.int32, (tq, tq), 0)
        tcol = lax.broadcasted_iota(jnp.int32, (tq, tq), 1)
        key_ref[chunk_rows(i), :] = jnp.where(srow <= tcol, key_ref[chunk_rows(i), :], INT_MIN)

        def pad_chunk(c, carry):
            key_ref[chunk_rows(c), :] = jnp.full((tq, tq), INT_MIN, jnp.int32)
            return carry

        lax.fori_loop(i + 1, (last_j + 1) * (tk // tq), pad_chunk, 0)

        def bit_step(b, prefix):
            bit = 31 - b
            trial_u = prefix | lax.shift_left(jnp.int32(1), bit)
            trial_s = trial_u ^ jnp.int32(INT_MIN)

            def count(c, cnt8):
                hit = jnp.where(key_ref[pl.ds(pl.multiple_of(c * tk, tk), tk), :] >= trial_s, 1, 0)
                return cnt8 + jnp.sum(hit.reshape(tk // 8, 8, tq), axis=0)

            cnt8 = lax.fori_loop(0, last_j + 1, count, jnp.zeros((8, tq), jnp.int32))
            cnt = jnp.sum(cnt8, axis=0, keepdims=True)
            return jnp.where(cnt >= topk, trial_u, prefix)

        prefix = lax.fori_loop(0, 32, bit_step, jnp.zeros((1, tq), jnp.int32))
        thr = prefix ^ jnp.int32(INT_MIN)
        thr_ref[...] = jnp.maximum(thr, INT_MIN + 1)
        _attn_init(m_ref, acc_ref)

    @pl.when(phase == 1)
    def _():
        sel = key_ref[pl.ds(pl.multiple_of(j * tk, tk), tk), :] >= thr_ref[...]
        delta = i * tq - j * tk
        near = delta - (tk - 1) < tq

        @pl.when(near)
        def _():
            diff = (lax.broadcasted_iota(jnp.int32, (tk, tq), 1)
                    - lax.broadcasted_iota(jnp.int32, (tk, tq), 0))
            bias_on = (diff >= -delta) & (diff < tq - delta)
            bias = jnp.where(bias_on[None], toep_ref[...], 0.0)
            _attn_tile(k_ref, qT_ref, vT_ref, m_ref, acc_ref, keep=sel, bias=bias)

        @pl.when(jnp.logical_not(near))
        def _():
            _attn_tile(k_ref, qT_ref, vT_ref, m_ref, acc_ref, keep=sel)

        @pl.when(j == last_j)
        def _():
            _attn_finish(o_ref, acc_ref)


def _bucket_table(n):
    max_exact = N_BUCKETS // 2
    d = np.arange(n)
    df = np.maximum(d.astype(np.float32), np.float32(1.0))
    large = max_exact + (np.log(df / max_exact) / math.log(MAX_DIST / max_exact)
                         * (N_BUCKETS - max_exact)).astype(np.int32)
    return np.where(d < max_exact, d, np.minimum(large, N_BUCKETS - 1)).astype(np.int32)


def _bucket_saturation():
    tab = _bucket_table(4 * MAX_DIST)
    assert tab[-1] == N_BUCKETS - 1 and np.all(np.diff(tab) >= 0)
    return int(np.argmax(tab == N_BUCKETS - 1))


def _dsa_prompt(k_i, q_iT, w_iT, k_pad, vT_aug, qT_pad, rel_bias, tq=256, tk=512):
    T = k_pad.shape[1]
    tq, tk = min(tq, T), min(tk, T)
    assert tk % tq == 0
    nq = T // tq
    topk = min(TOPK_MAX, T // 4)
    assert _bucket_saturation() <= tq
    tab = _bucket_table(tq)
    b = (rel_bias[tab, :] - rel_bias[N_BUCKETS - 1][None, :]) * LOG2E
    dmat = (np.arange(tq)[None, :] - np.arange(tk)[:, None]) % tq
    toep = jnp.transpose(b[dmat], (2, 0, 1)).astype(_F32)
    it, jt, ph = [], [], []
    for i in range(nq):
        it.append(i); jt.append(0); ph.append(0)
        for j in range(((i + 1) * tq - 1) // tk + 1):
            it.append(i); jt.append(j); ph.append(1)
    it, jt, ph = (np.asarray(a, np.int32) for a in (it, jt, ph))
    const = lambda shape: pl.BlockSpec(shape, lambda s, it, jt, ph: (0,) * len(shape),
                                       pipeline_mode=pl.Buffered(1))
    gs = pltpu.PrefetchScalarGridSpec(
        num_scalar_prefetch=3,
        grid=(len(it),),
        in_specs=[const((T, IDX_DIM)),
                  pl.BlockSpec((IDX_HEADS * IDX_DIM, tq), lambda s, it, jt, ph: (0, it[s])),
                  pl.BlockSpec((IDX_HEADS, tq), lambda s, it, jt, ph: (0, it[s])),
                  pl.BlockSpec((N_HEADS, tk, LANES), lambda s, it, jt, ph: (0, jt[s], 0)),
                  pl.BlockSpec((N_HEADS, V_ROWS, tk), lambda s, it, jt, ph: (0, 0, jt[s])),
                  pl.BlockSpec((N_HEADS, LANES, tq), lambda s, it, jt, ph: (0, 0, it[s])),
                  const((N_HEADS, tk, tq))],
        out_specs=pl.BlockSpec((N_HEADS * HEAD_DIM, tq), lambda s, it, jt, ph: (0, it[s])),
        scratch_shapes=[pltpu.VMEM((T, tq), jnp.int32), pltpu.VMEM((1, tq), jnp.int32),
                        pltpu.VMEM((N_HEADS, 8, tq), _F32), pltpu.VMEM((N_HEADS, V_ROWS, tq), _F32)],
    )
    return pl.pallas_call(
        functools.partial(_dsa_kernel, tq=tq, tk=tk, topk=topk),
        grid_spec=gs,
        out_shape=jax.ShapeDtypeStruct((N_HEADS * HEAD_DIM, T), _BF),
        compiler_params=_cparams(("arbitrary",)),
        name="dsa_prompt",
    )(jnp.asarray(it), jnp.asarray(jt), jnp.asarray(ph), k_i, q_iT, w_iT, k_pad, vT_aug, qT_pad, toep)


def _merge_kernel(oa_ref, ob_ref, ga_ref, gb_ref, x_ref, wa_ref, wb_ref, wo_ref, o_ref):
    ua = jnp.dot(oa_ref[...], wa_ref[...], preferred_element_type=_F32)
    ub = jnp.dot(ob_ref[...], wb_ref[...], preferred_element_type=_F32)
    merged = jax.nn.sigmoid(ga_ref[...]) * ua + jax.nn.sigmoid(gb_ref[...]) * ub
    o_ref[...] = x_ref[...] + jnp.dot(merged.astype(_BF), wo_ref[...], preferred_element_type=_F32)


def _merge(oa, ob, ga, gb, x, wa, wb, wo, tm):
    M, D = x.shape
    W = oa.shape[1]
    rowD = pl.BlockSpec((tm, D), lambda i: (i, 0))
    rowW = pl.BlockSpec((tm, W), lambda i: (i, 0))
    const = lambda shape: pl.BlockSpec(shape, lambda i: (0, 0), pipeline_mode=pl.Buffered(1))
    return pl.pallas_call(
        _merge_kernel,
        grid=(M // tm,),
        in_specs=[rowW, rowW, rowD, rowD, rowD, const((W, D)), const((W, D)), const((D, D))],
        out_specs=rowD,
        out_shape=jax.ShapeDtypeStruct((M, D), _F32),
        compiler_params=_cparams(("arbitrary",)),
        name="merge",
    )(oa, ob, ga, gb, x, wa.astype(_BF), wb.astype(_BF), wo.astype(_BF))


def _ffn_gate_down(y, x, wd_ref, dff):
    a = y[:, :dff]
    g = y[:, dff:]
    act = (g * jax.nn.sigmoid(g)) * a
    return x + jnp.dot(act.astype(_BF), wd_ref[...], preferred_element_type=_F32)


def _ffn_prompt_kernel(x_ref, g_ref, wu_ref, cw_ref, cb_ref, wd_ref, o_ref, st_ref, u_ref, *, tm, dff):
    @pl.when(pl.program_id(0) == 0)
    def _():
        u_ref[0:8, :] = jnp.zeros((8, 2 * dff), _F32)

    x = x_ref[...]
    h = _rms(x, g_ref[...]).astype(_BF)
    u_ref[8:8 + tm, :] = jnp.dot(h, wu_ref[...], preferred_element_type=_F32)
    y = (cb_ref[...] + cw_ref[0:1, :] * u_ref[6:6 + tm, :] + cw_ref[1:2, :] * u_ref[7:7 + tm, :]
         + cw_ref[2:3, :] * u_ref[8:8 + tm, :])
    o_ref[...] = _ffn_gate_down(y, x, wd_ref, dff)
    tail = u_ref[8 + tm - 2:8 + tm, :]
    st_ref[...] = tail
    u_ref[6:8, :] = tail


def _ffn_prompt(x, gamma, w_up, conv_w, conv_b, w_down, tm=256):
    T, D = x.shape
    dff = w_down.shape[0]
    const = lambda shape: pl.BlockSpec(shape, lambda i: (0, 0), pipeline_mode=pl.Buffered(1))
    return pl.pallas_call(
        functools.partial(_ffn_prompt_kernel, tm=tm, dff=dff),
        grid=(T // tm,),
        in_specs=[pl.BlockSpec((tm, D), lambda i: (i, 0)), const((1, D)), const((D, 2 * dff)),
                  const((3, 2 * dff)), const((1, 2 * dff)), const((dff, D))],
        out_specs=[pl.BlockSpec((tm, D), lambda i: (i, 0)), pl.BlockSpec((2, 2 * dff), lambda i: (0, 0))],
        out_shape=[jax.ShapeDtypeStruct((T, D), _F32), jax.ShapeDtypeStruct((2, 2 * dff), _F32)],
        scratch_shapes=[pltpu.VMEM((tm + 8, 2 * dff), _F32)],
        compiler_params=_cparams(("arbitrary",)),
        name="ffn_prompt",
    )(x, gamma.reshape(1, D), w_up.astype(_BF), conv_w, conv_b.reshape(1, -1), w_down.astype(_BF))


def _ffn_sample_kernel(x_ref, g_ref, wu_ref, cw_ref, cb_ref, wd_ref, s0_ref, s1_ref, o_ref, u_out_ref, *, dff):
    x = x_ref[...]
    h = _rms(x, g_ref[...]).astype(_BF)
    u = jnp.dot(h, wu_ref[...], preferred_element_type=_F32)
    y = cb_ref[...] + cw_ref[0:1, :] * s0_ref[...] + cw_ref[1:2, :] * s1_ref[...] + cw_ref[2:3, :] * u
    o_ref[...] = _ffn_gate_down(y, x, wd_ref, dff)
    u_out_ref[...] = u


def _ffn_sample(x, gamma, w_up, conv_w, conv_b, w_down, state):
    B, D = x.shape
    dff = w_down.shape[0]
    return pl.pallas_call(
        functools.partial(_ffn_sample_kernel, dff=dff),
        out_shape=[jax.ShapeDtypeStruct((B, D), _F32), jax.ShapeDtypeStruct((B, 2 * dff), _F32)],
        compiler_params=pltpu.CompilerParams(vmem_limit_bytes=VMEM_LIMIT),
        name="ffn_sample",
    )(x, gamma.reshape(1, D), w_up.astype(_BF), conv_w, conv_b.reshape(1, -1), w_down.astype(_BF),
      state[:, 0, :], state[:, 1, :])


def _norm_kernel(x_ref, g_ref, o_ref):
    o_ref[...] = _rms(x_ref[...], g_ref[...])


def _final_norm(x, gamma, tm):
    M, D = x.shape
    return pl.pallas_call(
        _norm_kernel,
        grid=(M // tm,),
        in_specs=[pl.BlockSpec((tm, D), lambda i: (i, 0)), pl.BlockSpec((1, D), lambda i: (0, 0))],
        out_specs=pl.BlockSpec((tm, D), lambda i: (i, 0)),
        out_shape=jax.ShapeDtypeStruct((M, D), _F32),
        compiler_params=_cparams(("arbitrary",)),
        name="final_norm",
    )(x, gamma.reshape(1, D))


def _proj_sample_kernel(x_ref, g_ref, w_ref, o_ref):
    h = _rms(x_ref[...], g_ref[...]).astype(_BF)
    o_ref[...] = jnp.dot(h, w_ref[...], preferred_element_type=_F32)


def _proj_sample(x, gamma, w_in):
    B, D = x.shape
    n = w_in.shape[1]
    npad = -(-n // LANES) * LANES
    w = jnp.pad(w_in, ((0, 0), (0, npad - n))).astype(_BF)
    z = pl.pallas_call(
        _proj_sample_kernel,
        out_shape=jax.ShapeDtypeStruct((B, npad), _F32),
        compiler_params=pltpu.CompilerParams(vmem_limit_bytes=VMEM_LIMIT),
        name="proj_sample",
    )(x, gamma.reshape(1, D), w)
    return z[:, :n]


def _logsig_kernel(f_ref, b_ref, o_ref):
    o_ref[...] = _log_sigmoid(f_ref[...] + b_ref[...])


def _log_sigmoid_rows(f, b):
    return pl.pallas_call(
        _logsig_kernel,
        out_shape=jax.ShapeDtypeStruct(f.shape, _F32),
        name="sample_logf",
    )(f, b.reshape(1, -1))


def _own_head_mask(n_lanes):
    row = lax.broadcasted_iota(jnp.int32, (N_HEADS, n_lanes), 0)
    lane = lax.broadcasted_iota(jnp.int32, (N_HEADS, n_lanes), 1)
    return (lane & (N_HEADS - 1)) == row


def _sample_scores_kernel(pt_sm, pool_ref, q_ref, w_ref, knew_ref, o_ref, onew_ref, buf, sem,
                          *, n_pages, n_batch, base):
    b = pl.program_id(0)

    def page_copy(bb, p, slot):
        return pltpu.make_async_copy(pool_ref.at[base + pt_sm[bb, p]], buf.at[slot, p], sem.at[slot])

    def start_all(bb, slot):
        def body(p, c):
            page_copy(bb, p, slot).start()
            return c
        lax.fori_loop(0, n_pages, body, 0)

    @pl.when(b == 0)
    def _():
        start_all(0, 0)

    @pl.when(b + 1 < n_batch)
    def _():
        start_all(b + 1, (b + 1) % 2)

    slot = b % 2

    def wait_body(p, c):
        page_copy(b, p, slot).wait()
        return c
    lax.fori_loop(0, n_pages, wait_body, 0)

    q = q_ref[0]
    w = w_ref[0]
    keys = buf[slot].reshape(n_pages * PAGE_SIZE, IDX_DIM).astype(_BF)
    s = _dot_nt(q, keys)
    o_ref[0] = jnp.sum(jnp.maximum(s, 0.0) * w, axis=0, keepdims=True)
    sn = _dot_nt(q, knew_ref[0].astype(_BF))
    onew_ref[0] = jnp.sum(jnp.maximum(sn, 0.0) * w, axis=0, keepdims=True)


def _sample_scores(cache_kidx, layer, page_table, q_idx, w_idx, k_new):
    depth, n_pool = cache_kidx.shape[:2]
    B, NP = page_table.shape
    P = NP * PAGE_SIZE
    pool = cache_kidx.reshape(depth * n_pool, PAGE_SIZE, IDX_DIM)
    idx_scale = (IDX_DIM ** -0.5) * (IDX_HEADS ** -0.5)
    knew8 = jnp.pad(k_new[:, None, :], ((0, 0), (0, 7), (0, 0)))
    per_b = lambda b, pt: (b, 0, 0)
    gs = pltpu.PrefetchScalarGridSpec(
        num_scalar_prefetch=1,
        grid=(B,),
        in_specs=[pl.BlockSpec(memory_space=pl.ANY),
                  pl.BlockSpec((1, IDX_HEADS, IDX_DIM), per_b),
                  pl.BlockSpec((1, IDX_HEADS, 1), per_b),
                  pl.BlockSpec((1, 8, IDX_DIM), per_b)],
        out_specs=[pl.BlockSpec((1, 1, P), per_b), pl.BlockSpec((1, 1, 8), per_b)],
        scratch_shapes=[pltpu.VMEM((2, NP, PAGE_SIZE, IDX_DIM), _F32), pltpu.SemaphoreType.DMA((2,))],
    )
    sc, sn = pl.pallas_call(
        functools.partial(_sample_scores_kernel, n_pages=NP, n_batch=B, base=layer * n_pool),
        grid_spec=gs,
        out_shape=[jax.ShapeDtypeStruct((B, 1, P), _F32), jax.ShapeDtypeStruct((B, 1, 8), _F32)],
        compiler_params=_cparams(("arbitrary",)),
        name="sample_scores",
    )(page_table, pool, q_idx.astype(_BF), (w_idx * idx_scale)[:, :, None], knew8)
    return jnp.concatenate([sc[:, 0, :], sn[:, 0, :1]], axis=1)


def _topk_kernel(x_ref, o_ref, buf_ref, *, k):
    B, L = x_ref.shape
    buf_ref[...] = x_ref[...]
    lane = lax.broadcasted_iota(jnp.int32, (B, L), 1).astype(_F32)
    col = lax.broadcasted_iota(jnp.int32, (B, k), 1)

    def body(r, out):
        x = buf_ref[...]
        m = jnp.max(x, axis=1, keepdims=True)
        idx = jnp.min(jnp.where(x == m, lane, float(L)), axis=1, keepdims=True)
        buf_ref[...] = jnp.where(lane == idx, -jnp.inf, x)
        return jnp.where(col == r, idx, out)

    o_ref[...] = lax.fori_loop(0, k, body, jnp.zeros((B, k), _F32)).astype(jnp.int32)


def _topk_indices(scores, k):
    B, n = scores.shape
    L = -(-n // LANES) * LANES
    x = jnp.pad(scores, ((0, 0), (0, L - n)), constant_values=-jnp.inf)
    return pl.pallas_call(
        functools.partial(_topk_kernel, k=k),
        out_shape=jax.ShapeDtypeStruct((B, k), jnp.int32),
        scratch_shapes=[pltpu.VMEM((B, L), _F32)],
        compiler_params=pltpu.CompilerParams(vmem_limit_bytes=VMEM_LIMIT),
        name="sample_topk",
    )(x)


def _bucket_edges():
    tab = _bucket_table(4 * MAX_DIST)
    max_exact = N_BUCKETS // 2
    edges = [int(np.argmax(tab >= b)) for b in range(max_exact + 1, N_BUCKETS)]
    for b, e in zip(range(max_exact + 1, N_BUCKETS), edges):
        assert tab[e] == b and tab[e - 1] == b - 1
    return edges


def _dsa_sample_kernel(idx_sm, pt_sm, kc_ref, vc_ref, knew_ref, vnew_ref, idx8_ref, q_ref, rbT_ref,
                       o_ref, kbuf, vbuf, sem, *, k, past, layer):
    b = pl.program_id(0)

    def pool_rows(r):
        sp = jnp.minimum(idx_sm[b, r], past - 1)
        page = pt_sm[b, lax.shift_right_logical(sp, int(math.log2(PAGE_SIZE)))]
        off = sp & (PAGE_SIZE - 1)
        return kc_ref.at[layer, page, off], vc_ref.at[layer, page, off]

    def copies(k_src, v_src, r):
        return (pltpu.make_async_copy(k_src, kbuf.at[r], sem.at[0]),
                pltpu.make_async_copy(v_src, vbuf.at[r], sem.at[1]))

    def start(r, c):
        is_new = idx_sm[b, r] >= past

        @pl.when(is_new)
        def _():
            for cp in copies(knew_ref.at[b], vnew_ref.at[b], r):
                cp.start()

        @pl.when(jnp.logical_not(is_new))
        def _():
            for cp in copies(*pool_rows(r), r):
                cp.start()
        return c

    lax.fori_loop(0, k, start, 0)

    def wait(r, c):
        for cp in copies(*pool_rows(r), r):
            cp.wait()
        return c

    lax.fori_loop(0, k, wait, 0)

    n = k * N_HEADS
    idx8 = idx8_ref[0]
    dist = past - idx8
    max_exact = N_BUCKETS // 2
    bucket = jnp.full(idx8.shape, max_exact, jnp.int32)
    for e in _bucket_edges():
        bucket = bucket + jnp.where(dist >= e, 1, 0)
    bucket = jnp.where(dist < max_exact, dist, bucket)
    onehot = (lax.broadcasted_iota(jnp.int32, (N_BUCKETS, n), 0) == bucket).astype(_BF)
    bias = _dot3(rbT_ref[...], onehot)
    keys = kbuf[...].reshape(n, HEAD_DIM).astype(_BF)
    vals = vbuf[...].reshape(n, HEAD_DIM).astype(_BF)
    logits = jnp.where(_own_head_mask(n), _dot_nt(q_ref[0], keys) + bias, NEG)
    m = jnp.max(logits, axis=1, keepdims=True)
    p = jnp.exp(logits - m)
    denom = jnp.sum(p, axis=1, keepdims=True)
    o_ref[0] = jnp.dot(p.astype(_BF), vals, preferred_element_type=_F32) / denom


def _dsa_sample(cache_k, cache_v, layer, page_table, idx, q, k_new, v_new, rel_bias):
    B, NP = page_table.shape
    K = idx.shape[1]
    hd = lambda t: t.reshape(B, N_HEADS, HEAD_DIM)
    qh = (hd(q) * (HEAD_DIM ** -0.5)).astype(_BF)
    idx8 = jnp.repeat(idx, N_HEADS, axis=1)[:, None, :]
    any_spec = pl.BlockSpec(memory_space=pl.ANY)
    per_b = lambda b, i, p: (b, 0, 0)
    gs = pltpu.PrefetchScalarGridSpec(
        num_scalar_prefetch=2,
        grid=(B,),
        in_specs=[any_spec, any_spec, any_spec, any_spec,
                  pl.BlockSpec((1, 1, K * N_HEADS), per_b),
                  pl.BlockSpec((1, N_HEADS, HEAD_DIM), per_b),
                  pl.BlockSpec((N_HEADS, N_BUCKETS), lambda b, i, p: (0, 0))],
        out_specs=pl.BlockSpec((1, N_HEADS, HEAD_DIM), per_b),
        scratch_shapes=[pltpu.VMEM((K, N_HEADS, HEAD_DIM), _F32), pltpu.VMEM((K, N_HEADS, HEAD_DIM), _F32),
                        pltpu.SemaphoreType.DMA((2,))],
    )
    return pl.pallas_call(
        functools.partial(_dsa_sample_kernel, k=K, past=NP * PAGE_SIZE, layer=layer),
        grid_spec=gs,
        out_shape=jax.ShapeDtypeStruct((B, N_HEADS, HEAD_DIM), _F32),
        compiler_params=_cparams(("arbitrary",)),
        name="dsa_sample",
    )(idx, page_table, cache_k, cache_v, hd(k_new), hd(v_new), idx8, qh, rel_bias.T)


def _fox_sample_kernel(pt_sm, kc_ref, vc_ref, lf_ref, q_ref, knew_ref, vnew_ref, lfnew_ref, upper_ref, same_ref,
                       o_ref, kbuf, vbuf, lfbuf, sem, m_ref, l_ref, acc_ref, c_ref,
                       *, n_groups, group, n_batch, layer, lf_base):
    b = pl.program_id(0)
    gi = pl.program_id(1)
    n = b * n_groups + gi
    page_lanes = PAGE_SIZE * N_HEADS

    def copies(bb, gg, slot):
        out = []
        for g in range(group):
            page = pt_sm[bb, gg * group + g]
            out.append(pltpu.make_async_copy(kc_ref.at[layer, page], kbuf.at[slot, g], sem.at[slot]))
            out.append(pltpu.make_async_copy(vc_ref.at[layer, page], vbuf.at[slot, g], sem.at[slot]))
            out.append(pltpu.make_async_copy(lf_ref.at[pl.ds(lf_base + page, 1), :],
                                             lfbuf.at[slot, pl.ds(g, 1), :], sem.at[slot]))
        return out

    @pl.when(n == 0)
    def _():
        for cp in copies(0, 0, 0):
            cp.start()

    @pl.when(n + 1 < n_batch * n_groups)
    def _():
        nxt = n + 1
        for cp in copies(nxt // n_groups, nxt % n_groups, nxt % 2):
            cp.start()

    slot = n % 2
    for cp in copies(b, gi, slot):
        cp.wait()

    @pl.when(gi == 0)
    def _():
        m_ref[...] = jnp.full(m_ref.shape, NEG, _F32)
        l_ref[...] = jnp.zeros(l_ref.shape, _F32)
        acc_ref[...] = jnp.zeros(acc_ref.shape, _F32)
        c_ref[...] = jnp.zeros(c_ref.shape, _F32)

    q = q_ref[0]
    lf = lfbuf[slot]
    within = _dot3(lf, upper_ref[...])
    totals = _dot3(lf, same_ref[...])
    carry = c_ref[...]
    pieces = []
    for g in range(group):
        pieces.append(carry + within[g:g + 1, :])
        carry = carry + totals[g:g + 1, :]
    c_ref[...] = carry
    csum = jnp.concatenate(pieces, axis=1)
    nl = group * page_lanes
    keys = kbuf[slot].reshape(nl, HEAD_DIM).astype(_BF)
    vals = vbuf[slot].reshape(nl, HEAD_DIM).astype(_BF)
    s = jnp.where(_own_head_mask(nl), _dot_nt(q, keys) - csum, NEG)
    m_old = m_ref[...]
    m_new = jnp.maximum(m_old, jnp.max(s, axis=1, keepdims=True))
    alpha = jnp.exp(m_old - m_new)
    pr = jnp.exp(s - m_new)
    l_ref[...] = alpha * l_ref[...] + jnp.sum(pr, axis=1, keepdims=True)
    acc_ref[...] = alpha * acc_ref[...] + jnp.dot(pr.astype(_BF), vals, preferred_element_type=_F32)
    m_ref[...] = m_new

    @pl.when(gi == n_groups - 1)
    def _():
        own = _own_head_mask(page_lanes) & (lax.broadcasted_iota(jnp.int32, (N_HEADS, page_lanes), 1) < N_HEADS)
        c_head = jnp.sum(jnp.where(own, c_ref[...], 0.0), axis=1, keepdims=True)
        c_new = c_head + lfnew_ref[0]
        s_new = jnp.sum(q.astype(_F32) * knew_ref[0], axis=1, keepdims=True) - c_new
        m_o = m_ref[...]
        m_n = jnp.maximum(m_o, s_new)
        a = jnp.exp(m_o - m_n)
        p_new = jnp.exp(s_new - m_n)
        l_fin = a * l_ref[...] + p_new
        o_ref[0] = (a * acc_ref[...] + p_new * vnew_ref[0]) / l_fin


def _fox_sample(cache_k, cache_v, cache_logf, layer, page_table, q, k_new, v_new, logf_new, group=8):
    depth, n_pool = cache_k.shape[:2]
    B, NP = page_table.shape
    group = min(group, NP)
    n_groups = NP // group
    page_lanes = PAGE_SIZE * N_HEADS
    lf2d = cache_logf.reshape(depth * n_pool, page_lanes)
    hd = lambda t: t.reshape(B, N_HEADS, HEAD_DIM)
    qh = (hd(q) * (HEAD_DIM ** -0.5)).astype(_BF)
    lane = np.arange(page_lanes)
    same = (lane[:, None] % N_HEADS) == (lane[None, :] % N_HEADS)
    upper = same & (lane[:, None] // N_HEADS <= lane[None, :] // N_HEADS)
    any_spec = pl.BlockSpec(memory_space=pl.ANY)
    per_b = lambda b, g, pt: (b, 0, 0)
    const = pl.BlockSpec((page_lanes, page_lanes), lambda b, g, pt: (0, 0), pipeline_mode=pl.Buffered(1))
    page_buf = pltpu.VMEM((2, group, PAGE_SIZE, N_HEADS, HEAD_DIM), _F32)
    gs = pltpu.PrefetchScalarGridSpec(
        num_scalar_prefetch=1,
        grid=(B, n_groups),
        in_specs=[any_spec, any_spec, any_spec,
                  pl.BlockSpec((1, N_HEADS, HEAD_DIM), per_b), pl.BlockSpec((1, N_HEADS, HEAD_DIM), per_b),
                  pl.BlockSpec((1, N_HEADS, HEAD_DIM), per_b), pl.BlockSpec((1, N_HEADS, 1), per_b),
                  const, const],
        out_specs=pl.BlockSpec((1, N_HEADS, HEAD_DIM), per_b),
        scratch_shapes=[page_buf, page_buf, pltpu.VMEM((2, group, page_lanes), _F32),
                        pltpu.SemaphoreType.DMA((2,)),
                        pltpu.VMEM((N_HEADS, 1), _F32), pltpu.VMEM((N_HEADS, 1), _F32),
                        pltpu.VMEM((N_HEADS, HEAD_DIM), _F32), pltpu.VMEM((1, page_lanes), _F32)],
    )
    return pl.pallas_call(
        functools.partial(_fox_sample_kernel, n_groups=n_groups, group=group, n_batch=B, layer=layer,
                          lf_base=layer * n_pool),
        grid_spec=gs,
        out_shape=jax.ShapeDtypeStruct((B, N_HEADS, HEAD_DIM), _F32),
        compiler_params=_cparams(("arbitrary", "arbitrary")),
        name="fox_sample",
    )(page_table, cache_k, cache_v, lf2d, qh, hd(k_new), hd(v_new), logf_new[:, :, None],
      jnp.asarray(upper, _BF), jnp.asarray(same, _BF))


def _prompt_attention_inputs(pp):
    T = pp["k_a"].shape[0]
    heads = lambda x: x.reshape(T, N_HEADS, HEAD_DIM).transpose(1, 0, 2).astype(_BF)
    pad_k = lambda k, cols: jnp.concatenate(
        [k] + cols + [jnp.zeros((N_HEADS, T, LANES - HEAD_DIM - len(cols)), _BF)], axis=-1)
    pad_q = lambda qT, n1: jnp.concatenate(
        [qT.reshape(N_HEADS, HEAD_DIM, T), jnp.ones((N_HEADS, n1, T), _BF),
         jnp.zeros((N_HEADS, LANES - HEAD_DIM - n1, T), _BF)], axis=1)
    c = _cumsum_time(pp["logfT"])
    cols = [x[:, :, None] for x in _split3(-LOG2E * c)]
    fox = (pad_k(heads(pp["k_b"]), cols), _values_with_ones(pp["v_bT"]), pad_q(pp["q_bT"], 3))
    dsa = (pad_k(heads(pp["k_a"]), []), _values_with_ones(pp["v_aT"]), pad_q(pp["q_aT"], 0))
    return fox, dsa


def kernel(x_prompt, x_sample, cache_k_a, cache_v_a, cache_kidx, cache_k_b, cache_v_b, cache_logf_b,
           state_conv, page_table, rel_bias, attn_norm, w_in, b_forget, w_a_up, w_b_up, w_o,
           ffn_norm, w_up, conv_w, conv_b, w_down, final_norm):
    depth = w_in.shape[0]
    Bp, T, D = x_prompt.shape
    Bs = x_sample.shape[0]
    assert Bp == 1 and x_sample.shape[1] == 1
    NP = page_table.shape[1]
    P = NP * PAGE_SIZE
    W = N_HEADS * HEAD_DIM
    topk_s = min(TOPK_MAX, (P + 1) // 4)
    tm = min(512, T)

    xp = x_prompt[0]
    xs = x_sample[:, 0, :]
    sizes = (W, W, W, IDX_HEADS * IDX_DIM, IDX_DIM, IDX_HEADS, W, W, W, N_HEADS, D, D)
    cuts = np.cumsum((0,) + sizes)
    new = [[] for _ in range(14)]
    for l in range(depth):
        pp = _proj_prompt(xp, attn_norm[l], w_in[l], b_forget[l])
        fox, dsa = _prompt_attention_inputs(pp)
        obT = _fox_prompt(*fox)
        oaT = _dsa_prompt(pp["k_i"][:, :IDX_DIM].astype(_BF), pp["q_iT"], pp["w_iT"], *dsa, rel_bias)
        xp = _merge(oaT.T, obT.T, pp["g_a"], pp["g_b"], xp, w_a_up[l], w_b_up[l], w_o[l], tm)
        xp, conv_p = _ffn_prompt(xp, ffn_norm[l], w_up[l], conv_w[l], conv_b[l], w_down[l])

        z = _proj_sample(xs, attn_norm[l], w_in[l])
        (q_a, k_a, v_a, q_i, k_i, w_i, q_b, k_b, v_b, f_b, g_a, g_b) = [z[:, cuts[k]:cuts[k + 1]] for k in range(12)]
        logf_s = _log_sigmoid_rows(f_b, b_forget[l])
        scores = _sample_scores(cache_kidx, l, page_table, q_i.reshape(Bs, IDX_HEADS, IDX_DIM), w_i, k_i)
        idx = _topk_indices(scores, topk_s)
        oa_s = _dsa_sample(cache_k_a, cache_v_a, l, page_table, idx, q_a, k_a, v_a, rel_bias)
        ob_s = _fox_sample(cache_k_b, cache_v_b, cache_logf_b, l, page_table, q_b, k_b, v_b, logf_s)
        flat = lambda o: o.reshape(Bs, W).astype(_BF)
        xs = _merge(flat(oa_s), flat(ob_s), g_a, g_b, xs, w_a_up[l], w_b_up[l], w_o[l], Bs)
        xs, u_s = _ffn_sample(xs, ffn_norm[l], w_up[l], conv_w[l], conv_b[l], w_down[l], state_conv[l])
        conv_s = jnp.stack([state_conv[l][:, 1, :], u_s], axis=1)

        hd = lambda t, n: t.reshape(t.shape[0], n, HEAD_DIM)
        rows = (hd(pp["k_a"], N_HEADS)[None], hd(k_a, N_HEADS)[:, None],
                hd(pp["v_a"], N_HEADS)[None], hd(v_a, N_HEADS)[:, None],
                pp["k_i"][None, :, :IDX_DIM], k_i[:, None],
                hd(pp["k_b"], N_HEADS)[None], hd(k_b, N_HEADS)[:, None],
                hd(pp["v_b"], N_HEADS)[None], hd(v_b, N_HEADS)[:, None],
                pp["logfT"].T[None], logf_s[:, None],
                conv_p[None], conv_s)
        for lst, r in zip(new, rows):
            lst.append(r)
    y_prompt = _final_norm(xp, final_norm, tm)[None]
    y_sample = _final_norm(xs, final_norm, Bs)[:, None]
    return (y_prompt, y_sample) + tuple(jnp.stack(n) for n in new)
```

```python
import functools
import math

import numpy as np
import jax
import jax.numpy as jnp
from jax import lax
from jax.experimental import pallas as pl
from jax.experimental.pallas import tpu as pltpu

HEAD_DIM = 64
N_HEADS = 8
IDX_HEADS = 8
IDX_DIM = 32
TOPK_MAX = 256
N_BUCKETS = 32
MAX_DIST = 128
PAGE_SIZE = 128
NORM_EPS = 1e-6
NEG = -1e30
INT_MIN = -2 ** 31
LANES = 128
V_ROWS = 80
VMEM_LIMIT = 56 * 1024 * 1024
LOG2E = math.log2(math.e)

_BF = jnp.bfloat16
_F32 = jnp.float32


def _cparams(sem):
    return pltpu.CompilerParams(dimension_semantics=sem, vmem_limit_bytes=VMEM_LIMIT)


def _split3(x):
    hi = x.astype(_BF)
    r1 = x - hi.astype(_F32)
    mid = r1.astype(_BF)
    lo = (r1 - mid.astype(_F32)).astype(_BF)
    return hi, mid, lo


def _dot3(x, b):
    hi, mid, lo = _split3(x)
    d = lambda a: jnp.dot(a, b, preferred_element_type=_F32)
    return d(hi) + d(mid) + d(lo)


def _rms(x, g):
    var = jnp.mean(x * x, axis=-1, keepdims=True)
    return x * lax.rsqrt(var + NORM_EPS) * g


def _log_sigmoid(x):
    return -(jnp.maximum(-x, 0.0) + jnp.log1p(jnp.exp(-jnp.abs(x))))


def _dot_nt(a, b):
    return lax.dot_general(a, b, (((1,), (1,)), ((), ())), preferred_element_type=_F32)


def _bdot(a, b):
    return lax.dot_general(a, b, (((2,), (1,)), ((0,), (0,))), preferred_element_type=_F32)


_NAT = (("k_a", 512), ("v_a", 512), ("k_b", 512), ("v_b", 512), ("g_a", 1024), ("g_b", 1024), ("k_i", 128))
_TRN = (("q_a", 512), ("q_b", 512), ("v_a", 512), ("v_b", 512), ("q_i", 256), ("w_i", 16), ("f_b", 16))


def _proj_prompt_kernel(x_ref, g_ref, wn_ref, wt_ref, bf_ref,
                        ka_ref, va_ref, kb_ref, vb_ref, ga_ref, gb_ref, ki_ref,
                        qaT_ref, qbT_ref, vaT_ref, vbT_ref, qiT_ref, wiT_ref, lfT_ref):
    h = _rms(x_ref[...], g_ref[...]).astype(_BF)
    nat_refs = (ka_ref, va_ref, kb_ref, vb_ref, ga_ref, gb_ref, ki_ref)
    off = 0
    for (name, n), ref in zip(_NAT, nat_refs):
        ref[...] = jnp.dot(h, wn_ref[:, off:off + n], preferred_element_type=_F32)
        off += n
    trn_refs = (qaT_ref, qbT_ref, vaT_ref, vbT_ref, qiT_ref, wiT_ref, lfT_ref)
    off = 0
    for (name, n), ref in zip(_TRN, trn_refs):
        z = _dot_nt(wt_ref[off:off + n, :], h)[:ref.shape[0]]
        if name == "f_b":
            z = _log_sigmoid(z + bf_ref[...])
        ref[...] = z.astype(ref.dtype)
        off += n


def _proj_prompt(x, gamma, w_in, b_f, tm=256):
    T, D = x.shape
    sizes = (512, 512, 512, 256, 32, 8, 512, 512, 512, 8, D, D)
    cuts = np.cumsum((0,) + sizes)
    col = lambda k: w_in[:, cuts[k]:cuts[k + 1]]
    q_a, k_a, v_a, q_i, k_i, w_i, q_b, k_b, v_b, f_b, g_a, g_b = [col(k) for k in range(12)]
    scale = (HEAD_DIM ** -0.5) * LOG2E
    idx_scale = (IDX_DIM ** -0.5) * (IDX_HEADS ** -0.5)
    k_i_pad = jnp.pad(k_i, ((0, 0), (0, LANES - IDX_DIM)))
    wn = jnp.concatenate([k_a, v_a, k_b, v_b, g_a, g_b, k_i_pad], axis=1).astype(_BF)
    pad8 = lambda w: jnp.pad(w, ((0, 0), (0, 8)))
    wt = jnp.concatenate([q_a * scale, q_b * scale, v_a, v_b, q_i, pad8(w_i * idx_scale), pad8(f_b)],
                         axis=1).T.astype(_BF)
    nn, nt = wn.shape[1], wt.shape[0]
    grid = (T // tm,)
    row = lambda n, dt=_F32: (jax.ShapeDtypeStruct((T, n), dt), pl.BlockSpec((tm, n), lambda i: (i, 0)))
    colT = lambda n, dt: (jax.ShapeDtypeStruct((n, T), dt), pl.BlockSpec((n, tm), lambda i: (0, i)))
    outs = [row(512), row(512), row(512), row(512), row(D), row(D), row(LANES),
            colT(512, _BF), colT(512, _BF), colT(512, _BF), colT(512, _BF), colT(256, _BF),
            colT(8, _F32), colT(8, _F32)]
    const = lambda shape: pl.BlockSpec(shape, lambda i: (0,) * len(shape), pipeline_mode=pl.Buffered(1))
    res = pl.pallas_call(
        _proj_prompt_kernel,
        grid=grid,
        in_specs=[pl.BlockSpec((tm, D), lambda i: (i, 0)), const((1, D)), const((D, nn)), const((nt, D)),
                  const((8, 1))],
        out_specs=[o[1] for o in outs],
        out_shape=[o[0] for o in outs],
        compiler_params=_cparams(("arbitrary",)),
        name="proj_prompt",
    )(x, gamma.reshape(1, D), wn, wt, b_f.reshape(8, 1))
    names = ("k_a", "v_a", "k_b", "v_b", "g_a", "g_b", "k_i",
             "q_aT", "q_bT", "v_aT", "v_bT", "q_iT", "w_iT", "logfT")
    return dict(zip(names, res))


def _cumsum_kernel(x_ref, lower_ref, o_ref):
    x = x_ref[...]
    n = x.shape[0]
    r = lax.broadcasted_iota(jnp.int32, (LANES, LANES), 0)
    c = lax.broadcasted_iota(jnp.int32, (LANES, LANES), 1)
    upper = (r <= c).astype(_BF)
    within = _dot3(x, upper)
    totals = within[:, LANES - 1:LANES]
    hi, mid, lo = _split3(totals)
    d = lambda a: jnp.dot(lower_ref[...], jnp.broadcast_to(a, (n, LANES)), preferred_element_type=_F32)
    o_ref[...] = within + (d(hi) + d(mid) + d(lo))


def _cumsum_time(logfT):
    H, T = logfT.shape
    C = T // LANES
    x = logfT.reshape(H * C, LANES)
    rows = np.arange(H * C)
    lower = (rows[:, None] // C == rows[None, :] // C) & (rows[None, :] < rows[:, None])
    out = pl.pallas_call(
        _cumsum_kernel,
        out_shape=jax.ShapeDtypeStruct((H * C, LANES), _F32),
        compiler_params=pltpu.CompilerParams(vmem_limit_bytes=VMEM_LIMIT),
        name="logf_cumsum",
    )(x, jnp.asarray(lower, _BF))
    return out.reshape(H, T)


def _attn_init(m_ref, acc_ref):
    m_ref[...] = jnp.full(m_ref.shape, NEG, _F32)
    acc_ref[...] = jnp.zeros(acc_ref.shape, _F32)


def _attn_tile(k_ref, qT_ref, vT_ref, m_ref, acc_ref, keep=None, bias=None):
    s = _bdot(k_ref[...], qT_ref[...])
    if bias is not None:
        s = s + bias
    if keep is not None:
        s = jnp.where(keep[None], s, NEG)
    m_old = m_ref[:, 0:1, :]
    m_new = jnp.maximum(m_old, jnp.max(s, axis=1, keepdims=True))
    alpha = jnp.exp2(m_old - m_new)
    p = jnp.exp2(s - m_new).astype(_BF)
    acc_ref[...] = alpha * acc_ref[...] + _bdot(vT_ref[...], p)
    m_ref[...] = jnp.broadcast_to(m_new, m_ref.shape)


def _attn_finish(o_ref, acc_ref):
    out = acc_ref[:, :HEAD_DIM, :] / acc_ref[:, HEAD_DIM:HEAD_DIM + 1, :]
    o_ref[...] = out.reshape(o_ref.shape).astype(o_ref.dtype)


def _values_with_ones(vT):
    T = vT.shape[1]
    v = vT.reshape(N_HEADS, HEAD_DIM, T)
    ones = jnp.ones((N_HEADS, 1, T), _BF)
    zeros = jnp.zeros((N_HEADS, V_ROWS - HEAD_DIM - 1, T), _BF)
    return jnp.concatenate([v, ones, zeros], axis=1)


def _fox_kernel(it_ref, jt_ref, k_ref, vT_ref, qT_ref, o_ref, m_ref, acc_ref, *, tq, tk):
    step = pl.program_id(0)
    i = it_ref[step]
    j = jt_ref[step]
    last_j = ((i + 1) * tq - 1) // tk

    @pl.when(j == 0)
    def _():
        _attn_init(m_ref, acc_ref)

    needs_mask = (j + 1) * tk - 1 > i * tq

    @pl.when(needs_mask)
    def _():
        srow = j * tk + lax.broadcasted_iota(jnp.int32, (tk, tq), 0)
        tcol = i * tq + lax.broadcasted_iota(jnp.int32, (tk, tq), 1)
        _attn_tile(k_ref, qT_ref, vT_ref, m_ref, acc_ref, keep=srow <= tcol)

    @pl.when(jnp.logical_not(needs_mask))
    def _():
        _attn_tile(k_ref, qT_ref, vT_ref, m_ref, acc_ref)

    @pl.when(j == last_j)
    def _():
        _attn_finish(o_ref, acc_ref)


def _causal_steps(nq, tq, tk):
    ii, jj = [], []
    for i in range(nq):
        for j in range(((i + 1) * tq - 1) // tk + 1):
            ii.append(i)
            jj.append(j)
    return np.asarray(ii, np.int32), np.asarray(jj, np.int32)


def _fox_prompt(k_aug, vT_aug, qT_aug, tq=512, tk=512):
    T = k_aug.shape[1]
    tq, tk = min(tq, T), min(tk, T)
    it, jt = _causal_steps(T // tq, tq, tk)
    gs = pltpu.PrefetchScalarGridSpec(
        num_scalar_prefetch=2,
        grid=(len(it),),
        in_specs=[pl.BlockSpec((N_HEADS, tk, LANES), lambda s, it, jt: (0, jt[s], 0)),
                  pl.BlockSpec((N_HEADS, V_ROWS, tk), lambda s, it, jt: (0, 0, jt[s])),
                  pl.BlockSpec((N_HEADS, LANES, tq), lambda s, it, jt: (0, 0, it[s]))],
        out_specs=pl.BlockSpec((N_HEADS * HEAD_DIM, tq), lambda s, it, jt: (0, it[s])),
        scratch_shapes=[pltpu.VMEM((N_HEADS, 8, tq), _F32), pltpu.VMEM((N_HEADS, V_ROWS, tq), _F32)],
    )
    return pl.pallas_call(
        functools.partial(_fox_kernel, tq=tq, tk=tk),
        grid_spec=gs,
        out_shape=jax.ShapeDtypeStruct((N_HEADS * HEAD_DIM, T), _BF),
        compiler_params=_cparams(("arbitrary",)),
        name="fox_prompt",
    )(jnp.asarray(it), jnp.asarray(jt), k_aug, vT_aug, qT_aug)


def _sort_key(x):
    b = lax.bitcast_convert_type(x, jnp.int32)
    return b ^ ((b >> 31) & jnp.int32(0x7FFFFFFF))


def _dsa_kernel(it_ref, jt_ref, ph_ref, ki_ref, qiT_ref, wiT_ref, k_ref, vT_ref, qT_ref, toep_ref,
                o_ref, key_ref, thr_ref, m_ref, acc_ref, *, tq, tk, topk):
    step = pl.program_id(0)
    i = it_ref[step]
    j = jt_ref[step]
    phase = ph_ref[step]
    last_j = ((i + 1) * tq - 1) // tk
    chunk_rows = lambda c: pl.ds(pl.multiple_of(c * tq, tq), tq)

    @pl.when(phase == 0)
    def _():
        w = wiT_ref[...]

        def chunk(c, carry):
            kc = ki_ref[chunk_rows(c), :]
            acc = jnp.zeros((tq, tq), _F32)
            for h in range(IDX_HEADS):
                s = jnp.dot(kc, qiT_ref[h * IDX_DIM:(h + 1) * IDX_DIM, :], preferred_element_type=_F32)
                acc = acc + jnp.maximum(s, 0.0) * w[h:h + 1, :]
            key_ref[chunk_rows(c), :] = _sort_key(acc)
            return carry

        lax.fori_loop(0, i, chunk, 0)
        chunk(i, 0)
        srow = lax.broadcasted_iota(jnp.int32, (tq, tq), 0)
        tcol = lax.broadcasted_iota(jnp.int32, (tq, tq), 1)
        key_ref[chunk_rows(i), :] = jnp.where(srow <= tcol, key_ref[chunk_rows(i), :], INT_MIN)

        def pad_chunk(c, carry):
            key_ref[chunk_rows(c), :] = jnp.full((tq, tq), INT_MIN, jnp.int32)
            return carry

        lax.fori_loop(i + 1, (last_j + 1) * (tk // tq), pad_chunk, 0)

        def bit_step(b, prefix):
            bit = 31 - b
            trial_u = prefix | lax.shift_left(jnp.int32(1), bit)
            trial_s = trial_u ^ jnp.int32(INT_MIN)

            def count(c, cnt8):
                hit = jnp.where(key_ref[pl.ds(pl.multiple_of(c * tk, tk), tk), :] >= trial_s, 1, 0)
                return cnt8 + jnp.sum(hit.reshape(tk // 8, 8, tq), axis=0)

            cnt8 = lax.fori_loop(0, last_j + 1, count, jnp.zeros((8, tq), jnp.int32))
            cnt = jnp.sum(cnt8, axis=0, keepdims=True)
            return jnp.where(cnt >= topk, trial_u, prefix)

        prefix = lax.fori_loop(0, 32, bit_step, jnp.zeros((1, tq), jnp.int32))
        thr = prefix ^ jnp.int32(INT_MIN)
        thr_ref[...] = jnp.maximum(thr, INT_MIN + 1)
        _attn_init(m_ref, acc_ref)

    @pl.when(phase == 1)
    def _():
        sel = key_ref[pl.ds(pl.multiple_of(j * tk, tk), tk), :] >= thr_ref[...]
        delta = i * tq - j * tk
        near = delta - (tk - 1) < tq

        @pl.when(near)
        def _():
            diff = (lax.broadcasted_iota(jnp.int32, (tk, tq), 1)
                    - lax.broadcasted_iota(jnp.int32, (tk, tq), 0))
            bias_on = (diff >= -delta) & (diff < tq - delta)
            bias = jnp.where(bias_on[None], toep_ref[...], 0.0)
            _attn_tile(k_ref, qT_ref, vT_ref, m_ref, acc_ref, keep=sel, bias=bias)

        @pl.when(jnp.logical_not(near))
        def _():
            _attn_tile(k_ref, qT_ref, vT_ref, m_ref, acc_ref, keep=sel)

        @pl.when(j == last_j)
        def _():
            _attn_finish(o_ref, acc_ref)


def _bucket_table(n):
    max_exact = N_BUCKETS // 2
    d = np.arange(n)
    df = np.maximum(d.astype(np.float32), np.float32(1.0))
    large = max_exact + (np.log(df / max_exact) / math.log(MAX_DIST / max_exact)
                         * (N_BUCKETS - max_exact)).astype(np.int32)
    return np.where(d < max_exact, d, np.minimum(large, N_BUCKETS - 1)).astype(np.int32)


def _bucket_saturation():
    tab = _bucket_table(4 * MAX_DIST)
    assert tab[-1] == N_BUCKETS - 1 and np.all(np.diff(tab) >= 0)
    return int(np.argmax(tab == N_BUCKETS - 1))


def _dsa_prompt(k_i, q_iT, w_iT, k_pad, vT_aug, qT_pad, rel_bias, tq=256, tk=512):
    T = k_pad.shape[1]
    tq, tk = min(tq, T), min(tk, T)
    assert tk % tq == 0
    nq = T // tq
    topk = min(TOPK_MAX, T // 4)
    assert _bucket_saturation() <= tq
    tab = _bucket_table(tq)
    b = (rel_bias[tab, :] - rel_bias[N_BUCKETS - 1][None, :]) * LOG2E
    dmat = (np.arange(tq)[None, :] - np.arange(tk)[:, None]) % tq
    toep = jnp.transpose(b[dmat], (2, 0, 1)).astype(_F32)
    it, jt, ph = [], [], []
    for i in range(nq):
        it.append(i); jt.append(0); ph.append(0)
        for j in range(((i + 1) * tq - 1) // tk + 1):
            it.append(i); jt.append(j); ph.append(1)
    it, jt, ph = (np.asarray(a, np.int32) for a in (it, jt, ph))
    const = lambda shape: pl.BlockSpec(shape, lambda s, it, jt, ph: (0,) * len(shape),
                                       pipeline_mode=pl.Buffered(1))
    gs = pltpu.PrefetchScalarGridSpec(
        num_scalar_prefetch=3,
        grid=(len(it),),
        in_specs=[const((T, IDX_DIM)),
                  pl.BlockSpec((IDX_HEADS * IDX_DIM, tq), lambda s, it, jt, ph: (0, it[s])),
                  pl.BlockSpec((IDX_HEADS, tq), lambda s, it, jt, ph: (0, it[s])),
                  pl.BlockSpec((N_HEADS, tk, LANES), lambda s, it, jt, ph: (0, jt[s], 0)),
                  pl.BlockSpec((N_HEADS, V_ROWS, tk), lambda s, it, jt, ph: (0, 0, jt[s])),
                  pl.BlockSpec((N_HEADS, LANES, tq), lambda s, it, jt, ph: (0, 0, it[s])),
                  const((N_HEADS, tk, tq))],
        out_specs=pl.BlockSpec((N_HEADS * HEAD_DIM, tq), lambda s, it, jt, ph: (0, it[s])),
        scratch_shapes=[pltpu.VMEM((T, tq), jnp.int32), pltpu.VMEM((1, tq), jnp.int32),
                        pltpu.VMEM((N_HEADS, 8, tq), _F32), pltpu.VMEM((N_HEADS, V_ROWS, tq), _F32)],
    )
    return pl.pallas_call(
        functools.partial(_dsa_kernel, tq=tq, tk=tk, topk=topk),
        grid_spec=gs,
        out_shape=jax.ShapeDtypeStruct((N_HEADS * HEAD_DIM, T), _BF),
        compiler_params=_cparams(("arbitrary",)),
        name="dsa_prompt",
    )(jnp.asarray(it), jnp.asarray(jt), jnp.asarray(ph), k_i, q_iT, w_iT, k_pad, vT_aug, qT_pad, toep)


def _merge_kernel(oa_ref, ob_ref, ga_ref, gb_ref, x_ref, wa_ref, wb_ref, wo_ref, o_ref):
    ua = jnp.dot(oa_ref[...], wa_ref[...], preferred_element_type=_F32)
    ub = jnp.dot(ob_ref[...], wb_ref[...], preferred_element_type=_F32)
    merged = jax.nn.sigmoid(ga_ref[...]) * ua + jax.nn.sigmoid(gb_ref[...]) * ub
    o_ref[...] = x_ref[...] + jnp.dot(merged.astype(_BF), wo_ref[...], preferred_element_type=_F32)


def _merge(oa, ob, ga, gb, x, wa, wb, wo, tm):
    M, D = x.shape
    W = oa.shape[1]
    rowD = pl.BlockSpec((tm, D), lambda i: (i, 0))
    rowW = pl.BlockSpec((tm, W), lambda i: (i, 0))
    const = lambda shape: pl.BlockSpec(shape, lambda i: (0, 0), pipeline_mode=pl.Buffered(1))
    return pl.pallas_call(
        _merge_kernel,
        grid=(M // tm,),
        in_specs=[rowW, rowW, rowD, rowD, rowD, const((W, D)), const((W, D)), const((D, D))],
        out_specs=rowD,
        out_shape=jax.ShapeDtypeStruct((M, D), _F32),
        compiler_params=_cparams(("arbitrary",)),
        name="merge",
    )(oa, ob, ga, gb, x, wa.astype(_BF), wb.astype(_BF), wo.astype(_BF))


def _ffn_gate_down(y, x, wd_ref, dff):
    a = y[:, :dff]
    g = y[:, dff:]
    act = (g * jax.nn.sigmoid(g)) * a
    return x + jnp.dot(act.astype(_BF), wd_ref[...], preferred_element_type=_F32)


def _ffn_prompt_kernel(x_ref, g_ref, wu_ref, cw_ref, cb_ref, wd_ref, o_ref, st_ref, u_ref, *, tm, dff):
    @pl.when(pl.program_id(0) == 0)
    def _():
        u_ref[0:8, :] = jnp.zeros((8, 2 * dff), _F32)

    x = x_ref[...]
    h = _rms(x, g_ref[...]).astype(_BF)
    u_ref[8:8 + tm, :] = jnp.dot(h, wu_ref[...], preferred_element_type=_F32)
    y = (cb_ref[...] + cw_ref[0:1, :] * u_ref[6:6 + tm, :] + cw_ref[1:2, :] * u_ref[7:7 + tm, :]
         + cw_ref[2:3, :] * u_ref[8:8 + tm, :])
    o_ref[...] = _ffn_gate_down(y, x, wd_ref, dff)
    tail = u_ref[8 + tm - 2:8 + tm, :]
    st_ref[...] = tail
    u_ref[6:8, :] = tail


def _ffn_prompt(x, gamma, w_up, conv_w, conv_b, w_down, tm=256):
    T, D = x.shape
    dff = w_down.shape[0]
    const = lambda shape: pl.BlockSpec(shape, lambda i: (0, 0), pipeline_mode=pl.Buffered(1))
    return pl.pallas_call(
        functools.partial(_ffn_prompt_kernel, tm=tm, dff=dff),
        grid=(T // tm,),
        in_specs=[pl.BlockSpec((tm, D), lambda i: (i, 0)), const((1, D)), const((D, 2 * dff)),
                  const((3, 2 * dff)), const((1, 2 * dff)), const((dff, D))],
        out_specs=[pl.BlockSpec((tm, D), lambda i: (i, 0)), pl.BlockSpec((2, 2 * dff), lambda i: (0, 0))],
        out_shape=[jax.ShapeDtypeStruct((T, D), _F32), jax.ShapeDtypeStruct((2, 2 * dff), _F32)],
        scratch_shapes=[pltpu.VMEM((tm + 8, 2 * dff), _F32)],
        compiler_params=_cparams(("arbitrary",)),
        name="ffn_prompt",
    )(x, gamma.reshape(1, D), w_up.astype(_BF), conv_w, conv_b.reshape(1, -1), w_down.astype(_BF))


def _ffn_sample_kernel(x_ref, g_ref, wu_ref, cw_ref, cb_ref, wd_ref, s0_ref, s1_ref, o_ref, u_out_ref, *, dff):
    x = x_ref[...]
    h = _rms(x, g_ref[...]).astype(_BF)
    u = jnp.dot(h, wu_ref[...], preferred_element_type=_F32)
    y = cb_ref[...] + cw_ref[0:1, :] * s0_ref[...] + cw_ref[1:2, :] * s1_ref[...] + cw_ref[2:3, :] * u
    o_ref[...] = _ffn_gate_down(y, x, wd_ref, dff)
    u_out_ref[...] = u


def _ffn_sample(x, gamma, w_up, conv_w, conv_b, w_down, state):
    B, D = x.shape
    dff = w_down.shape[0]
    return pl.pallas_call(
        functools.partial(_ffn_sample_kernel, dff=dff),
        out_shape=[jax.ShapeDtypeStruct((B, D), _F32), jax.ShapeDtypeStruct((B, 2 * dff), _F32)],
        compiler_params=pltpu.CompilerParams(vmem_limit_bytes=VMEM_LIMIT),
        name="ffn_sample",
    )(x, gamma.reshape(1, D), w_up.astype(_BF), conv_w, conv_b.reshape(1, -1), w_down.astype(_BF),
      state[:, 0, :], state[:, 1, :])


def _norm_kernel(x_ref, g_ref, o_ref):
    o_ref[...] = _rms(x_ref[...], g_ref[...])


def _final_norm(x, gamma, tm):
    M, D = x.shape
    return pl.pallas_call(
        _norm_kernel,
        grid=(M // tm,),
        in_specs=[pl.BlockSpec((tm, D), lambda i: (i, 0)), pl.BlockSpec((1, D), lambda i: (0, 0))],
        out_specs=pl.BlockSpec((tm, D), lambda i: (i, 0)),
        out_shape=jax.ShapeDtypeStruct((M, D), _F32),
        compiler_params=_cparams(("arbitrary",)),
        name="final_norm",
    )(x, gamma.reshape(1, D))


def _proj_sample_kernel(x_ref, g_ref, w_ref, o_ref):
    h = _rms(x_ref[...], g_ref[...]).astype(_BF)
    o_ref[...] = jnp.dot(h, w_ref[...], preferred_element_type=_F32)


def _proj_sample(x, gamma, w_in):
    B, D = x.shape
    n = w_in.shape[1]
    npad = -(-n // LANES) * LANES
    w = jnp.pad(w_in, ((0, 0), (0, npad - n))).astype(_BF)
    z = pl.pallas_call(
        _proj_sample_kernel,
        out_shape=jax.ShapeDtypeStruct((B, npad), _F32),
        compiler_params=pltpu.CompilerParams(vmem_limit_bytes=VMEM_LIMIT),
        name="proj_sample",
    )(x, gamma.reshape(1, D), w)
    return z[:, :n]


def _logsig_kernel(f_ref, b_ref, o_ref):
    o_ref[...] = _log_sigmoid(f_ref[...] + b_ref[...])


def _log_sigmoid_rows(f, b):
    return pl.pallas_call(
        _logsig_kernel,
        out_shape=jax.ShapeDtypeStruct(f.shape, _F32),
        name="sample_logf",
    )(f, b.reshape(1, -1))


def _pages_last(cache):
    nd = cache.ndim
    return jnp.transpose(cache, (0, 1) + tuple(range(3, nd)) + (2,))


SCORE_GROUP = 8


def _sample_scores_kernel(pt_sm, pool_ref, q8_ref, w8_ref, knew_ref, o_ref, onew_ref, buf, sem,
                          *, n_pages, n_batch, layer):
    b = pl.program_id(0)

    def page_copy(bb, p, slot):
        return pltpu.make_async_copy(pool_ref.at[layer, pt_sm[bb, p]], buf.at[slot, p], sem.at[slot])

    def start_all(bb, slot):
        def body(p, c):
            page_copy(bb, p, slot).start()
            return c
        lax.fori_loop(0, n_pages, body, 0)

    @pl.when(b == 0)
    def _():
        start_all(0, 0)

    @pl.when(b + 1 < n_batch)
    def _():
        start_all(b + 1, (b + 1) % 2)

    slot = b % 2

    def wait_body(p, c):
        page_copy(b, p, slot).wait()
        return c
    lax.fori_loop(0, n_pages, wait_body, 0)

    q8 = q8_ref[0]
    w8 = w8_ref[0]
    gsz = SCORE_GROUP
    for g in range(n_pages // gsz):
        keys = buf[slot, g * gsz:(g + 1) * gsz].reshape(gsz * IDX_DIM, PAGE_SIZE).astype(_BF)
        s = jnp.maximum(jnp.dot(q8, keys, preferred_element_type=_F32), 0.0) * w8
        o_ref[0, g * gsz:(g + 1) * gsz, :] = jnp.sum(s.reshape(gsz, IDX_HEADS, PAGE_SIZE), axis=1)
    q = q8[0:IDX_HEADS, 0:IDX_DIM]
    sn = _dot_nt(q, knew_ref[0].astype(_BF))
    onew_ref[0] = jnp.sum(jnp.maximum(sn, 0.0) * w8[0:IDX_HEADS], axis=0, keepdims=True)


def _sample_scores(cache_kidx, layer, page_table, q_idx, w_idx, k_new):
    B, NP = page_table.shape
    gsz = SCORE_GROUP
    assert NP % gsz == 0
    pool = _pages_last(cache_kidx)
    idx_scale = (IDX_DIM ** -0.5) * (IDX_HEADS ** -0.5)
    eye = jnp.eye(gsz, dtype=q_idx.dtype)
    q8 = (eye[None, :, None, :, None] * q_idx[:, None, :, None, :]).reshape(B, gsz * IDX_HEADS, gsz * IDX_DIM)
    w8 = jnp.tile(w_idx * idx_scale, (1, gsz))[:, :, None]
    knew8 = jnp.pad(k_new[:, None, :], ((0, 0), (0, 7), (0, 0)))
    per_b = lambda b, pt: (b, 0, 0)
    gs = pltpu.PrefetchScalarGridSpec(
        num_scalar_prefetch=1,
        grid=(B,),
        in_specs=[pl.BlockSpec(memory_space=pl.ANY),
                  pl.BlockSpec((1, gsz * IDX_HEADS, gsz * IDX_DIM), per_b),
                  pl.BlockSpec((1, gsz * IDX_HEADS, 1), per_b),
                  pl.BlockSpec((1, 8, IDX_DIM), per_b)],
        out_specs=[pl.BlockSpec((1, NP, PAGE_SIZE), per_b), pl.BlockSpec((1, 1, 8), per_b)],
        scratch_shapes=[pltpu.VMEM((2, NP, IDX_DIM, PAGE_SIZE), _F32), pltpu.SemaphoreType.DMA((2,))],
    )
    sc, sn = pl.pallas_call(
        functools.partial(_sample_scores_kernel, n_pages=NP, n_batch=B, layer=layer),
        grid_spec=gs,
        out_shape=[jax.ShapeDtypeStruct((B, NP, PAGE_SIZE), _F32), jax.ShapeDtypeStruct((B, 1, 8), _F32)],
        compiler_params=_cparams(("arbitrary",)),
        name="sample_scores",
    )(page_table, pool, q8.astype(_BF), w8, knew8)
    return sc, sn[:, 0, 0]


def _topk_kernel(x_ref, o_ref, buf_ref, *, k):
    B, L = x_ref.shape
    buf_ref[...] = x_ref[...]
    lane = lax.broadcasted_iota(jnp.int32, (B, L), 1).astype(_F32)

    def body(r, c):
        x = buf_ref[...]
        m = jnp.max(x, axis=1, keepdims=True)
        idx = jnp.min(jnp.where(x == m, lane, float(L)), axis=1, keepdims=True)
        buf_ref[...] = jnp.where(lane == idx, -jnp.inf, x)
        return c

    lax.fori_loop(0, k, body, 0)
    o_ref[...] = jnp.where(buf_ref[...] != x_ref[...], 1.0, 0.0)


def _topk_mask(scores, k):
    B, L = scores.shape
    return pl.pallas_call(
        functools.partial(_topk_kernel, k=k),
        out_shape=jax.ShapeDtypeStruct((B, L), _F32),
        scratch_shapes=[pltpu.VMEM((B, L), _F32)],
        compiler_params=pltpu.CompilerParams(vmem_limit_bytes=VMEM_LIMIT),
        name="sample_topk",
    )(scores)


def _bucket_edges():
    tab = _bucket_table(4 * MAX_DIST)
    max_exact = N_BUCKETS // 2
    edges = [int(np.argmax(tab >= b)) for b in range(max_exact + 1, N_BUCKETS)]
    for b, e in zip(range(max_exact + 1, N_BUCKETS), edges):
        assert tab[e] == b and tab[e - 1] == b - 1
    return edges


def _bias_rows(dist, rbT):
    max_exact = N_BUCKETS // 2
    bucket = jnp.full(dist.shape, max_exact, jnp.int32)
    for e in _bucket_edges():
        bucket = bucket + jnp.where(dist >= e, 1, 0)
    bucket = jnp.where(dist < max_exact, dist, bucket)
    onehot = (lax.broadcasted_iota(jnp.int32, (N_BUCKETS, dist.shape[1]), 0) == bucket).astype(_BF)
    return _dot3(rbT, onehot)


def _head_block_mask():
    r = lax.broadcasted_iota(jnp.int32, (N_HEADS, N_HEADS * HEAD_DIM), 0)
    c = lax.broadcasted_iota(jnp.int32, (N_HEADS, N_HEADS * HEAD_DIM), 1)
    return (c // HEAD_DIM) == r


def _paged_attn_kernel(pt_sm, kc_ref, vc_ref, *rest, forget, n_groups, group, n_batch, layer, past):
    if forget:
        (lf_ref, qbd_ref, knew_ref, vnew_ref, lfnew_ref, upper_ref,
         o_ref, kbuf, vbuf, lfbuf, sem, m_ref, l_ref, acc_ref, c_ref) = rest
    else:
        (sel_ref, selnew_ref, qbd_ref, knew_ref, vnew_ref, rbT_ref,
         o_ref, kbuf, vbuf, sem, m_ref, l_ref, acc_ref, bias_ref) = rest
    b = pl.program_id(0)
    gi = pl.program_id(1)
    n = b * n_groups + gi
    W = N_HEADS * HEAD_DIM

    def copies(bb, gg, slot):
        out = []
        for g in range(group):
            page = pt_sm[bb, gg * group + g]
            out.append(pltpu.make_async_copy(kc_ref.at[layer, page], kbuf.at[slot, g], sem.at[slot]))
            out.append(pltpu.make_async_copy(vc_ref.at[layer, page], vbuf.at[slot, g], sem.at[slot]))
            if forget:
                out.append(pltpu.make_async_copy(lf_ref.at[layer, page], lfbuf.at[slot, g], sem.at[slot]))
        return out

    @pl.when(n == 0)
    def _():
        for cp in copies(0, 0, 0):
            cp.start()

    @pl.when(n + 1 < n_batch * n_groups)
    def _():
        nxt = n + 1
        for cp in copies(nxt // n_groups, nxt % n_groups, nxt % 2):
            cp.start()

    slot = n % 2
    for cp in copies(b, gi, slot):
        cp.wait()

    @pl.when(gi == 0)
    def _():
        m_ref[...] = jnp.full(m_ref.shape, NEG, _F32)
        l_ref[...] = jnp.zeros(l_ref.shape, _F32)
        acc_ref[...] = jnp.zeros(acc_ref.shape, _F32)
        if forget:
            c_ref[...] = jnp.zeros(c_ref.shape, _F32)
        else:
            bias_ref[...] = jnp.broadcast_to(rbT_ref[:, N_BUCKETS - 1:N_BUCKETS], bias_ref.shape)

    if not forget:
        @pl.when(gi == n_groups - 1)
        def _():
            lane = lax.broadcasted_iota(jnp.int32, (1, group * PAGE_SIZE), 1)
            dist = past - (gi * group * PAGE_SIZE + lane)
            bias_ref[...] = _bias_rows(dist, rbT_ref[...])

    qbd = qbd_ref[0]
    if forget:
        upper = upper_ref[...]
        carry = c_ref[...]
    pieces = []
    for g in range(group):
        keys = kbuf[slot, g].reshape(W, PAGE_SIZE).astype(_BF)
        s = jnp.dot(qbd, keys, preferred_element_type=_F32)
        if forget:
            csum = carry + _dot3(lfbuf[slot, g], upper)
            carry = csum[:, PAGE_SIZE - 1:PAGE_SIZE]
            s = s - csum
        pieces.append(s)
    s = jnp.concatenate(pieces, axis=1)
    if forget:
        c_ref[...] = carry
    else:
        sel = sel_ref[0].reshape(1, group * PAGE_SIZE)
        s = jnp.where(sel > 0.0, s + bias_ref[...], NEG)
    m_old = m_ref[...]
    m_new = jnp.maximum(m_old, jnp.max(s, axis=1, keepdims=True))
    alpha = jnp.exp(m_old - m_new)
    pr = jnp.exp(s - m_new).astype(_BF)
    l_ref[...] = alpha * l_ref[...] + jnp.sum(pr.astype(_F32), axis=1, keepdims=True)
    pv = jnp.zeros((N_HEADS, W), _F32)
    for g in range(group):
        vals = vbuf[slot, g].reshape(W, PAGE_SIZE).astype(_BF)
        pv = pv + _dot_nt(pr[:, g * PAGE_SIZE:(g + 1) * PAGE_SIZE], vals)
    acc_ref[...] = alpha * acc_ref[...] + pv
    m_ref[...] = m_new

    @pl.when(gi == n_groups - 1)
    def _():
        s_new = jnp.sum(qbd.astype(_F32) * knew_ref[0], axis=1, keepdims=True)
        m_o = m_ref[...]
        if forget:
            s_new = s_new - (c_ref[...] + lfnew_ref[0])
            m_n = jnp.maximum(m_o, s_new)
            p_new = jnp.exp(s_new - m_n)
        else:
            picked = selnew_ref[0][:, 0:1] > 0.0
            s_new = s_new + rbT_ref[:, 0:1]
            m_n = jnp.where(picked, jnp.maximum(m_o, s_new), m_o)
            p_new = jnp.where(picked, jnp.exp(s_new - m_n), 0.0)
        a = jnp.exp(m_o - m_n)
        l_fin = a * l_ref[...] + p_new
        full = (a * acc_ref[...] + p_new * vnew_ref[0]) / l_fin
        o_ref[0] = jnp.sum(jnp.where(_head_block_mask(), full, 0.0), axis=0, keepdims=True)


def _block_diag_q(q):
    B = q.shape[0]
    qh = (q * (HEAD_DIM ** -0.5)).reshape(B, N_HEADS, 1, HEAD_DIM)
    eye = jnp.eye(N_HEADS, dtype=q.dtype)[None, :, :, None]
    return (qh * eye).reshape(B, N_HEADS, N_HEADS * HEAD_DIM).astype(_BF)


def _paged_attention(cache_k, cache_v, layer, page_table, q, k_new, v_new, *, logf=None, logf_new=None,
                     sel=None, sel_new=None, rel_bias=None, group=16):
    forget = logf is not None
    B, NP = page_table.shape
    group = min(group, NP)
    assert NP % group == 0
    n_groups = NP // group
    W = N_HEADS * HEAD_DIM
    kT = _pages_last(cache_k)
    vT = _pages_last(cache_v)
    any_spec = pl.BlockSpec(memory_space=pl.ANY)
    per_b = lambda b, g, pt: (b, 0, 0)
    common_in = [pl.BlockSpec((1, N_HEADS, W), per_b), pl.BlockSpec((1, 1, W), per_b), pl.BlockSpec((1, 1, W), per_b)]
    common_args = (_block_diag_q(q), k_new[:, None, :], v_new[:, None, :])
    page_buf = pltpu.VMEM((2, group, N_HEADS, HEAD_DIM, PAGE_SIZE), _F32)
    stats = [pltpu.VMEM((N_HEADS, 1), _F32), pltpu.VMEM((N_HEADS, 1), _F32), pltpu.VMEM((N_HEADS, W), _F32)]
    if forget:
        lane = np.arange(PAGE_SIZE)
        upper = jnp.asarray(lane[:, None] <= lane[None, :], _BF)
        in_specs = [any_spec, any_spec, any_spec] + common_in + [
            pl.BlockSpec((1, N_HEADS, 1), per_b), pl.BlockSpec((PAGE_SIZE, PAGE_SIZE), lambda b, g, pt: (0, 0))]
        args = (kT, vT, _pages_last(logf)) + common_args + (logf_new[:, :, None], upper)
        scratch = [page_buf, page_buf, pltpu.VMEM((2, group, N_HEADS, PAGE_SIZE), _F32),
                   pltpu.SemaphoreType.DMA((2,))] + stats + [pltpu.VMEM((N_HEADS, 1), _F32)]
    else:
        in_specs = [any_spec, any_spec,
                    pl.BlockSpec((1, group, PAGE_SIZE), lambda b, g, pt: (b, g, 0)),
                    pl.BlockSpec((1, 1, PAGE_SIZE), per_b)] + common_in + [
            pl.BlockSpec((N_HEADS, N_BUCKETS), lambda b, g, pt: (0, 0))]
        args = (kT, vT, sel, sel_new) + common_args + (rel_bias.T,)
        scratch = [page_buf, page_buf, pltpu.SemaphoreType.DMA((2,))] + stats + [
            pltpu.VMEM((N_HEADS, group * PAGE_SIZE), _F32)]
    gs = pltpu.PrefetchScalarGridSpec(
        num_scalar_prefetch=1,
        grid=(B, n_groups),
        in_specs=in_specs,
        out_specs=pl.BlockSpec((1, 1, W), per_b),
        scratch_shapes=scratch,
    )
    out = pl.pallas_call(
        functools.partial(_paged_attn_kernel, forget=forget, n_groups=n_groups, group=group, n_batch=B,
                          layer=layer, past=NP * PAGE_SIZE),
        grid_spec=gs,
        out_shape=jax.ShapeDtypeStruct((B, 1, W), _F32),
        compiler_params=_cparams(("arbitrary", "arbitrary")),
        name="fox_sample" if forget else "dsa_sample",
    )(page_table, *args)
    return out[:, 0, :]


def _prompt_attention_inputs(pp):
    T = pp["k_a"].shape[0]
    heads = lambda x: x.reshape(T, N_HEADS, HEAD_DIM).transpose(1, 0, 2).astype(_BF)
    pad_k = lambda k, cols: jnp.concatenate(
        [k] + cols + [jnp.zeros((N_HEADS, T, LANES - HEAD_DIM - len(cols)), _BF)], axis=-1)
    pad_q = lambda qT, n1: jnp.concatenate(
        [qT.reshape(N_HEADS, HEAD_DIM, T), jnp.ones((N_HEADS, n1, T), _BF),
         jnp.zeros((N_HEADS, LANES - HEAD_DIM - n1, T), _BF)], axis=1)
    c = _cumsum_time(pp["logfT"])
    cols = [x[:, :, None] for x in _split3(-LOG2E * c)]
    fox = (pad_k(heads(pp["k_b"]), cols), _values_with_ones(pp["v_bT"]), pad_q(pp["q_bT"], 3))
    dsa = (pad_k(heads(pp["k_a"]), []), _values_with_ones(pp["v_aT"]), pad_q(pp["q_aT"], 0))
    return fox, dsa


def kernel(x_prompt, x_sample, cache_k_a, cache_v_a, cache_kidx, cache_k_b, cache_v_b, cache_logf_b,
           state_conv, page_table, rel_bias, attn_norm, w_in, b_forget, w_a_up, w_b_up, w_o,
           ffn_norm, w_up, conv_w, conv_b, w_down, final_norm):
    depth = w_in.shape[0]
    Bp, T, D = x_prompt.shape
    Bs = x_sample.shape[0]
    assert Bp == 1 and x_sample.shape[1] == 1
    NP = page_table.shape[1]
    P = NP * PAGE_SIZE
    W = N_HEADS * HEAD_DIM
    topk_s = min(TOPK_MAX, (P + 1) // 4)
    tm = min(512, T)

    xp = x_prompt[0]
    xs = x_sample[:, 0, :]
    sizes = (W, W, W, IDX_HEADS * IDX_DIM, IDX_DIM, IDX_HEADS, W, W, W, N_HEADS, D, D)
    cuts = np.cumsum((0,) + sizes)
    new = [[] for _ in range(14)]
    for l in range(depth):
        pp = _proj_prompt(xp, attn_norm[l], w_in[l], b_forget[l])
        fox, dsa = _prompt_attention_inputs(pp)
        obT = _fox_prompt(*fox)
        oaT = _dsa_prompt(pp["k_i"][:, :IDX_DIM].astype(_BF), pp["q_iT"], pp["w_iT"], *dsa, rel_bias)
        xp = _merge(oaT.T, obT.T, pp["g_a"], pp["g_b"], xp, w_a_up[l], w_b_up[l], w_o[l], tm)
        xp, conv_p = _ffn_prompt(xp, ffn_norm[l], w_up[l], conv_w[l], conv_b[l], w_down[l])

        z = _proj_sample(xs, attn_norm[l], w_in[l])
        (q_a, k_a, v_a, q_i, k_i, w_i, q_b, k_b, v_b, f_b, g_a, g_b) = [z[:, cuts[k]:cuts[k + 1]] for k in range(12)]
        logf_s = _log_sigmoid_rows(f_b, b_forget[l])
        sc, sn = _sample_scores(cache_kidx, l, page_table, q_i.reshape(Bs, IDX_HEADS, IDX_DIM), w_i, k_i)
        pad = jnp.full((Bs, PAGE_SIZE - 1), -jnp.inf, _F32)
        scores = jnp.concatenate([sc.reshape(Bs, P), sn[:, None], pad], axis=1)
        sel = _topk_mask(scores, topk_s).reshape(Bs, NP + 1, PAGE_SIZE)
        oa_s = _paged_attention(cache_k_a, cache_v_a, l, page_table, q_a, k_a, v_a,
                                sel=sel[:, :NP], sel_new=sel[:, NP:], rel_bias=rel_bias)
        ob_s = _paged_attention(cache_k_b, cache_v_b, l, page_table, q_b, k_b, v_b,
                                logf=cache_logf_b, logf_new=logf_s)
        xs = _merge(oa_s.astype(_BF), ob_s.astype(_BF), g_a, g_b, xs, w_a_up[l], w_b_up[l], w_o[l], Bs)
        xs, u_s = _ffn_sample(xs, ffn_norm[l], w_up[l], conv_w[l], conv_b[l], w_down[l], state_conv[l])
        conv_s = jnp.stack([state_conv[l][:, 1, :], u_s], axis=1)

        hd = lambda t, n: t.reshape(t.shape[0], n, HEAD_DIM)
        rows = (hd(pp["k_a"], N_HEADS)[None], hd(k_a, N_HEADS)[:, None],
                hd(pp["v_a"], N_HEADS)[None], hd(v_a, N_HEADS)[:, None],
                pp["k_i"][None, :, :IDX_DIM], k_i[:, None],
                hd(pp["k_b"], N_HEADS)[None], hd(k_b, N_HEADS)[:, None],
                hd(pp["v_b"], N_HEADS)[None], hd(v_b, N_HEADS)[:, None],
                pp["logfT"].T[None], logf_s[:, None],
                conv_p[None], conv_s)
        for lst, r in zip(new, rows):
            lst.append(r)
    y_prompt = _final_norm(xp, final_norm, tm)[None]
    y_sample = _final_norm(xs, final_norm, Bs)[:, None]
    return (y_prompt, y_sample) + tuple(jnp.stack(n) for n in new)
```

```python
import functools
import math

import numpy as np
import jax
import jax.numpy as jnp
from jax import lax
from jax.experimental import pallas as pl
from jax.experimental.pallas import tpu as pltpu

HEAD_DIM = 64
N_HEADS = 8
IDX_HEADS = 8
IDX_DIM = 32
TOPK_MAX = 256
N_BUCKETS = 32
MAX_DIST = 128
PAGE_SIZE = 128
NORM_EPS = 1e-6
NEG = -1e30
INT_MIN = -2 ** 31
LANES = 128
V_ROWS = 80
VMEM_LIMIT = 56 * 1024 * 1024
LOG2E = math.log2(math.e)

_BF = jnp.bfloat16
_F32 = jnp.float32


def _cparams(sem):
    return pltpu.CompilerParams(dimension_semantics=sem, vmem_limit_bytes=VMEM_LIMIT)


def _split3(x):
    hi = x.astype(_BF)
    r1 = x - hi.astype(_F32)
    mid = r1.astype(_BF)
    lo = (r1 - mid.astype(_F32)).astype(_BF)
    return hi, mid, lo


def _dot3(x, b):
    hi, mid, lo = _split3(x)
    d = lambda a: jnp.dot(a, b, preferred_element_type=_F32)
    return d(hi) + d(mid) + d(lo)


def _rms(x, g):
    var = jnp.mean(x * x, axis=-1, keepdims=True)
    return x * lax.rsqrt(var + NORM_EPS) * g


def _log_sigmoid(x):
    return -(jnp.maximum(-x, 0.0) + jnp.log1p(jnp.exp(-jnp.abs(x))))


def _dot_nt(a, b):
    return lax.dot_general(a, b, (((1,), (1,)), ((), ())), preferred_element_type=_F32)


def _bdot(a, b):
    return lax.dot_general(a, b, (((2,), (1,)), ((0,), (0,))), preferred_element_type=_F32)


_NAT = (("k_a", 512), ("v_a", 512), ("k_b", 512), ("v_b", 512), ("g_a", 1024), ("g_b", 1024), ("k_i", 128))
_TRN = (("q_a", 512), ("q_b", 512), ("v_a", 512), ("v_b", 512), ("q_i", 256), ("w_i", 16), ("f_b", 16))


def _proj_prompt_kernel(x_ref, g_ref, wn_ref, wt_ref, bf_ref,
                        ka_ref, va_ref, kb_ref, vb_ref, ga_ref, gb_ref, ki_ref,
                        qaT_ref, qbT_ref, vaT_ref, vbT_ref, qiT_ref, wiT_ref, lfT_ref):
    h = _rms(x_ref[...], g_ref[...]).astype(_BF)
    nat_refs = (ka_ref, va_ref, kb_ref, vb_ref, ga_ref, gb_ref, ki_ref)
    off = 0
    for (name, n), ref in zip(_NAT, nat_refs):
        ref[...] = jnp.dot(h, wn_ref[:, off:off + n], preferred_element_type=_F32)
        off += n
    trn_refs = (qaT_ref, qbT_ref, vaT_ref, vbT_ref, qiT_ref, wiT_ref, lfT_ref)
    off = 0
    for (name, n), ref in zip(_TRN, trn_refs):
        z = _dot_nt(wt_ref[off:off + n, :], h)[:ref.shape[0]]
        if name == "f_b":
            z = _log_sigmoid(z + bf_ref[...])
        ref[...] = z.astype(ref.dtype)
        off += n


def _proj_prompt(x, gamma, w_in, b_f, tm=256):
    T, D = x.shape
    sizes = (512, 512, 512, 256, 32, 8, 512, 512, 512, 8, D, D)
    cuts = np.cumsum((0,) + sizes)
    col = lambda k: w_in[:, cuts[k]:cuts[k + 1]]
    q_a, k_a, v_a, q_i, k_i, w_i, q_b, k_b, v_b, f_b, g_a, g_b = [col(k) for k in range(12)]
    scale = (HEAD_DIM ** -0.5) * LOG2E
    idx_scale = (IDX_DIM ** -0.5) * (IDX_HEADS ** -0.5)
    k_i_pad = jnp.pad(k_i, ((0, 0), (0, LANES - IDX_DIM)))
    wn = jnp.concatenate([k_a, v_a, k_b, v_b, g_a, g_b, k_i_pad], axis=1).astype(_BF)
    pad8 = lambda w: jnp.pad(w, ((0, 0), (0, 8)))
    wt = jnp.concatenate([q_a * scale, q_b * scale, v_a, v_b, q_i, pad8(w_i * idx_scale), pad8(f_b)],
                         axis=1).T.astype(_BF)
    nn, nt = wn.shape[1], wt.shape[0]
    grid = (T // tm,)
    row = lambda n, dt=_F32: (jax.ShapeDtypeStruct((T, n), dt), pl.BlockSpec((tm, n), lambda i: (i, 0)))
    colT = lambda n, dt: (jax.ShapeDtypeStruct((n, T), dt), pl.BlockSpec((n, tm), lambda i: (0, i)))
    outs = [row(512), row(512), row(512), row(512), row(D), row(D), row(LANES),
            colT(512, _BF), colT(512, _BF), colT(512, _BF), colT(512, _BF), colT(256, _BF),
            colT(8, _F32), colT(8, _F32)]
    const = lambda shape: pl.BlockSpec(shape, lambda i: (0,) * len(shape), pipeline_mode=pl.Buffered(1))
    res = pl.pallas_call(
        _proj_prompt_kernel,
        grid=grid,
        in_specs=[pl.BlockSpec((tm, D), lambda i: (i, 0)), const((1, D)), const((D, nn)), const((nt, D)),
                  const((8, 1))],
        out_specs=[o[1] for o in outs],
        out_shape=[o[0] for o in outs],
        compiler_params=_cparams(("arbitrary",)),
        name="proj_prompt",
    )(x, gamma.reshape(1, D), wn, wt, b_f.reshape(8, 1))
    names = ("k_a", "v_a", "k_b", "v_b", "g_a", "g_b", "k_i",
             "q_aT", "q_bT", "v_aT", "v_bT", "q_iT", "w_iT", "logfT")
    return dict(zip(names, res))


def _cumsum_kernel(x_ref, lower_ref, o_ref):
    x = x_ref[...]
    n = x.shape[0]
    r = lax.broadcasted_iota(jnp.int32, (LANES, LANES), 0)
    c = lax.broadcasted_iota(jnp.int32, (LANES, LANES), 1)
    upper = (r <= c).astype(_BF)
    within = _dot3(x, upper)
    totals = within[:, LANES - 1:LANES]
    hi, mid, lo = _split3(totals)
    d = lambda a: jnp.dot(lower_ref[...], jnp.broadcast_to(a, (n, LANES)), preferred_element_type=_F32)
    o_ref[...] = within + (d(hi) + d(mid) + d(lo))


def _cumsum_time(logfT):
    H, T = logfT.shape
    C = T // LANES
    x = logfT.reshape(H * C, LANES)
    rows = np.arange(H * C)
    lower = (rows[:, None] // C == rows[None, :] // C) & (rows[None, :] < rows[:, None])
    out = pl.pallas_call(
        _cumsum_kernel,
        out_shape=jax.ShapeDtypeStruct((H * C, LANES), _F32),
        compiler_params=pltpu.CompilerParams(vmem_limit_bytes=VMEM_LIMIT),
        name="logf_cumsum",
    )(x, jnp.asarray(lower, _BF))
    return out.reshape(H, T)


def _attn_init(m_ref, acc_ref):
    m_ref[...] = jnp.full(m_ref.shape, NEG, _F32)
    acc_ref[...] = jnp.zeros(acc_ref.shape, _F32)


def _attn_tile(k_ref, qT_ref, vT_ref, m_ref, acc_ref, keep=None, bias=None):
    s = _bdot(k_ref[...], qT_ref[...])
    if bias is not None:
        s = s + bias
    if keep is not None:
        s = jnp.where(keep[None], s, NEG)
    m_old = m_ref[:, 0:1, :]
    m_new = jnp.maximum(m_old, jnp.max(s, axis=1, keepdims=True))
    alpha = jnp.exp2(m_old - m_new)
    p = jnp.exp2(s - m_new).astype(_BF)
    acc_ref[...] = alpha * acc_ref[...] + _bdot(vT_ref[...], p)
    m_ref[...] = jnp.broadcast_to(m_new, m_ref.shape)


def _attn_finish(o_ref, acc_ref):
    out = acc_ref[:, :HEAD_DIM, :] / acc_ref[:, HEAD_DIM:HEAD_DIM + 1, :]
    o_ref[...] = out.reshape(o_ref.shape).astype(o_ref.dtype)


def _values_with_ones(vT):
    T = vT.shape[1]
    v = vT.reshape(N_HEADS, HEAD_DIM, T)
    ones = jnp.ones((N_HEADS, 1, T), _BF)
    zeros = jnp.zeros((N_HEADS, V_ROWS - HEAD_DIM - 1, T), _BF)
    return jnp.concatenate([v, ones, zeros], axis=1)


def _fox_kernel(it_ref, jt_ref, skip_ref, jf_ref, k_ref, vT_ref, qT_ref, o_ref, m_ref, acc_ref, *, tq, tk):
    step = pl.program_id(0)
    i = it_ref[step]
    j = jt_ref[step]
    last_j = ((i + 1) * tq - 1) // tk

    @pl.when(j == 0)
    def _():
        _attn_init(m_ref, acc_ref)

    needs_mask = (j + 1) * tk - 1 > i * tq
    active = skip_ref[step] == 0

    @pl.when(needs_mask)
    def _():
        srow = j * tk + lax.broadcasted_iota(jnp.int32, (tk, tq), 0)
        tcol = i * tq + lax.broadcasted_iota(jnp.int32, (tk, tq), 1)
        _attn_tile(k_ref, qT_ref, vT_ref, m_ref, acc_ref, keep=srow <= tcol)

    @pl.when(jnp.logical_not(needs_mask) & active)
    def _():
        _attn_tile(k_ref, qT_ref, vT_ref, m_ref, acc_ref)

    @pl.when(j == last_j)
    def _():
        _attn_finish(o_ref, acc_ref)


def _causal_steps(nq, tq, tk):
    ii, jj = [], []
    for i in range(nq):
        for j in range(((i + 1) * tq - 1) // tk + 1):
            ii.append(i)
            jj.append(j)
    return np.asarray(ii, np.int32), np.asarray(jj, np.int32)


SKIP_MARGIN = 168.0


def _fox_skip_table(k_aug, qT_aug, it, jt, tq, tk):
    f32 = lambda x: x.astype(_F32)
    kk, ee = f32(k_aug[:, :, :HEAD_DIM]), jnp.sum(f32(k_aug[:, :, HEAD_DIM:HEAD_DIM + 3]), axis=-1)
    kn = jnp.sqrt(jnp.sum(kk * kk, axis=-1)) * 1.01
    qq = f32(qT_aug[:, :HEAD_DIM, :])
    qn = jnp.sqrt(jnp.sum(qq * qq, axis=1)) * 1.01
    H, T = kn.shape
    blk = lambda x, n, red: red(x.reshape(H, T // n, n), axis=-1)
    upper = blk(ee, tk, jnp.max)[:, :, None] + blk(kn, tk, jnp.max)[:, :, None] * blk(qn, tq, jnp.max)[:, None, :]
    lower = blk(ee - qn * kn, tq, jnp.min)[:, None, :]
    skip = jnp.all(upper + SKIP_MARGIN < lower, axis=0)
    diag = (jt + 1) * tk - 1 > it * tq
    skip_steps = jnp.where(jnp.asarray(diag), False, skip[jt, it])
    n = len(it)
    idx = jnp.where(skip_steps, n, jnp.arange(n))
    nxt = lax.cummin(idx[::-1])[::-1]
    jfetch = jnp.asarray(jt)[jnp.minimum(nxt, n - 1)]
    return skip_steps.astype(jnp.int32), jfetch.astype(jnp.int32)


def _fox_prompt(k_aug, vT_aug, qT_aug, tq=512, tk=512):
    T = k_aug.shape[1]
    tq, tk = min(tq, T), min(tk, T)
    it, jt = _causal_steps(T // tq, tq, tk)
    skip, jfetch = _fox_skip_table(k_aug, qT_aug, it, jt, tq, tk)
    gs = pltpu.PrefetchScalarGridSpec(
        num_scalar_prefetch=4,
        grid=(len(it),),
        in_specs=[pl.BlockSpec((N_HEADS, tk, LANES), lambda s, it, jt, sk, jf: (0, jf[s], 0)),
                  pl.BlockSpec((N_HEADS, V_ROWS, tk), lambda s, it, jt, sk, jf: (0, 0, jf[s])),
                  pl.BlockSpec((N_HEADS, LANES, tq), lambda s, it, jt, sk, jf: (0, 0, it[s]))],
        out_specs=pl.BlockSpec((N_HEADS * HEAD_DIM, tq), lambda s, it, jt, sk, jf: (0, it[s])),
        scratch_shapes=[pltpu.VMEM((N_HEADS, 8, tq), _F32), pltpu.VMEM((N_HEADS, V_ROWS, tq), _F32)],
    )
    return pl.pallas_call(
        functools.partial(_fox_kernel, tq=tq, tk=tk),
        grid_spec=gs,
        out_shape=jax.ShapeDtypeStruct((N_HEADS * HEAD_DIM, T), _BF),
        compiler_params=_cparams(("arbitrary",)),
        name="fox_prompt",
    )(jnp.asarray(it), jnp.asarray(jt), skip, jfetch, k_aug, vT_aug, qT_aug)


def _sort_key(x):
    b = lax.bitcast_convert_type(x, jnp.int32)
    return b ^ ((b >> 31) & jnp.int32(0x7FFFFFFF))


def _dsa_kernel(it_ref, jt_ref, ph_ref, ki_ref, qiT_ref, wiT_ref, k_ref, vT_ref, qT_ref, toep_ref,
                o_ref, key_ref, thr_ref, m_ref, acc_ref, *, tq, tk, topk):
    step = pl.program_id(0)
    i = it_ref[step]
    j = jt_ref[step]
    phase = ph_ref[step]
    last_j = ((i + 1) * tq - 1) // tk
    chunk_rows = lambda c: pl.ds(pl.multiple_of(c * tq, tq), tq)

    @pl.when(phase == 0)
    def _():
        w = wiT_ref[...]

        def chunk(c, carry):
            kc = ki_ref[chunk_rows(c), :]
            acc = jnp.zeros((tq, tq), _F32)
            for h in range(IDX_HEADS):
                s = jnp.dot(kc, qiT_ref[h * IDX_DIM:(h + 1) * IDX_DIM, :], preferred_element_type=_F32)
                acc = acc + jnp.maximum(s, 0.0) * w[h:h + 1, :]
            key_ref[chunk_rows(c), :] = _sort_key(acc)
            return carry

        lax.fori_loop(0, i, chunk, 0)
        chunk(i, 0)
        srow = lax.broadcasted_iota(jnp.int32, (tq, tq), 0)
        tcol = lax.broadcasted_iota(jnp.int32, (tq, tq), 1)
        key_ref[chunk_rows(i), :] = jnp.where(srow <= tcol, key_ref[chunk_rows(i), :], INT_MIN)

        def pad_chunk(c, carry):
            key_ref[chunk_rows(c), :] = jnp.full((tq, tq), INT_MIN, jnp.int32)
            return carry

        lax.fori_loop(i + 1, (last_j + 1) * (tk // tq), pad_chunk, 0)

        def bit_step(b, prefix):
            bit = 31 - b
            trial_u = prefix | lax.shift_left(jnp.int32(1), bit)
            trial_s = trial_u ^ jnp.int32(INT_MIN)

            def count(c, cnt8):
                hit = jnp.where(key_ref[pl.ds(pl.multiple_of(c * tk, tk), tk), :] >= trial_s, 1, 0)
                return cnt8 + jnp.sum(hit.reshape(tk // 8, 8, tq), axis=0)

            cnt8 = lax.fori_loop(0, last_j + 1, count, jnp.zeros((8, tq), jnp.int32))
            cnt = jnp.sum(cnt8, axis=0, keepdims=True)
            return jnp.where(cnt >= topk, trial_u, prefix)

        prefix = lax.fori_loop(0, 32, bit_step, jnp.zeros((1, tq), jnp.int32))
        thr = jnp.maximum(prefix ^ jnp.int32(INT_MIN), INT_MIN + 1)
        thr_ref[...] = thr

        def block_rows(c):
            return pl.ds(pl.multiple_of(c * tk, tk), tk)

        def count_where(pred):
            def body(c, cnt8):
                hit = jnp.where(pred(c, key_ref[block_rows(c), :]), 1, 0)
                return cnt8 + jnp.sum(hit.reshape(tk // 8, 8, tq), axis=0)
            cnt8 = lax.fori_loop(0, last_j + 1, body, jnp.zeros((8, tq), jnp.int32))
            return jnp.sum(cnt8, axis=0, keepdims=True)

        excess = count_where(lambda c, blk: blk >= thr) - topk

        @pl.when(jnp.max(excess) > 0)
        def _():
            row = lambda c: c * tk + lax.broadcasted_iota(jnp.int32, (tk, tq), 0)
            n_tied = count_where(lambda c, blk: blk == thr)
            keep_n = jnp.where(excess > 0, n_tied - excess, jnp.int32(2 ** 30))
            n_bits = max(1, int(math.ceil(math.log2(key_ref.shape[0] + 1))))

            def pos_step(b, cut):
                cand = cut | lax.shift_left(jnp.int32(1), n_bits - 1 - b)
                before = count_where(lambda c, blk: (blk == thr) & (row(c) < cand))
                return jnp.where(before < keep_n, cand, cut)

            cut = lax.fori_loop(0, n_bits, pos_step, jnp.zeros((1, tq), jnp.int32))

            def demote(c, carry):
                blk = key_ref[block_rows(c), :]
                key_ref[block_rows(c), :] = jnp.where((blk == thr) & (row(c) > cut), INT_MIN, blk)
                return carry

            lax.fori_loop(0, last_j + 1, demote, 0)

        _attn_init(m_ref, acc_ref)

    @pl.when(phase == 1)
    def _():
        sel = key_ref[pl.ds(pl.multiple_of(j * tk, tk), tk), :] >= thr_ref[...]
        delta = i * tq - j * tk
        near = delta - (tk - 1) < tq

        @pl.when(near)
        def _():
            diff = (lax.broadcasted_iota(jnp.int32, (tk, tq), 1)
                    - lax.broadcasted_iota(jnp.int32, (tk, tq), 0))
            bias_on = (diff >= -delta) & (diff < tq - delta)
            bias = jnp.where(bias_on[None], toep_ref[...], 0.0)
            _attn_tile(k_ref, qT_ref, vT_ref, m_ref, acc_ref, keep=sel, bias=bias)

        @pl.when(jnp.logical_not(near))
        def _():
            _attn_tile(k_ref, qT_ref, vT_ref, m_ref, acc_ref, keep=sel)

        @pl.when(j == last_j)
        def _():
            _attn_finish(o_ref, acc_ref)


def _bucket_table(n):
    max_exact = N_BUCKETS // 2
    d = np.arange(n)
    df = np.maximum(d.astype(np.float32), np.float32(1.0))
    large = max_exact + (np.log(df / max_exact) / math.log(MAX_DIST / max_exact)
                         * (N_BUCKETS - max_exact)).astype(np.int32)
    return np.where(d < max_exact, d, np.minimum(large, N_BUCKETS - 1)).astype(np.int32)


def _bucket_saturation():
    tab = _bucket_table(4 * MAX_DIST)
    assert tab[-1] == N_BUCKETS - 1 and np.all(np.diff(tab) >= 0)
    return int(np.argmax(tab == N_BUCKETS - 1))


def _dsa_prompt(k_i, q_iT, w_iT, k_pad, vT_aug, qT_pad, rel_bias, tq=256, tk=512):
    T = k_pad.shape[1]
    tq, tk = min(tq, T), min(tk, T)
    assert tk % tq == 0
    nq = T // tq
    topk = min(TOPK_MAX, T // 4)
    assert _bucket_saturation() <= tq
    tab = _bucket_table(tq)
    b = (rel_bias[tab, :] - rel_bias[N_BUCKETS - 1][None, :]) * LOG2E
    dmat = (np.arange(tq)[None, :] - np.arange(tq)[:, None]) % tq
    toep = jnp.tile(jnp.transpose(b[dmat], (2, 0, 1)).astype(_F32), (1, tk // tq, 1))
    it, jt, ph = [], [], []
    for i in range(nq):
        it.append(i); jt.append(0); ph.append(0)
        for j in range(((i + 1) * tq - 1) // tk + 1):
            it.append(i); jt.append(j); ph.append(1)
    it, jt, ph = (np.asarray(a, np.int32) for a in (it, jt, ph))
    const = lambda shape: pl.BlockSpec(shape, lambda s, it, jt, ph: (0,) * len(shape),
                                       pipeline_mode=pl.Buffered(1))
    gs = pltpu.PrefetchScalarGridSpec(
        num_scalar_prefetch=3,
        grid=(len(it),),
        in_specs=[const((T, IDX_DIM)),
                  pl.BlockSpec((IDX_HEADS * IDX_DIM, tq), lambda s, it, jt, ph: (0, it[s])),
                  pl.BlockSpec((IDX_HEADS, tq), lambda s, it, jt, ph: (0, it[s])),
                  pl.BlockSpec((N_HEADS, tk, LANES), lambda s, it, jt, ph: (0, jt[s], 0)),
                  pl.BlockSpec((N_HEADS, V_ROWS, tk), lambda s, it, jt, ph: (0, 0, jt[s])),
                  pl.BlockSpec((N_HEADS, LANES, tq), lambda s, it, jt, ph: (0, 0, it[s])),
                  const((N_HEADS, tk, tq))],
        out_specs=pl.BlockSpec((N_HEADS * HEAD_DIM, tq), lambda s, it, jt, ph: (0, it[s])),
        scratch_shapes=[pltpu.VMEM((T, tq), jnp.int32), pltpu.VMEM((1, tq), jnp.int32),
                        pltpu.VMEM((N_HEADS, 8, tq), _F32), pltpu.VMEM((N_HEADS, V_ROWS, tq), _F32)],
    )
    return pl.pallas_call(
        functools.partial(_dsa_kernel, tq=tq, tk=tk, topk=topk),
        grid_spec=gs,
        out_shape=jax.ShapeDtypeStruct((N_HEADS * HEAD_DIM, T), _BF),
        compiler_params=_cparams(("arbitrary",)),
        name="dsa_prompt",
    )(jnp.asarray(it), jnp.asarray(jt), jnp.asarray(ph), k_i, q_iT, w_iT, k_pad, vT_aug, qT_pad, toep)


def _merge_kernel(oa_ref, ob_ref, ga_ref, gb_ref, x_ref, wa_ref, wb_ref, wo_ref, o_ref):
    ua = jnp.dot(oa_ref[...], wa_ref[...], preferred_element_type=_F32)
    ub = jnp.dot(ob_ref[...], wb_ref[...], preferred_element_type=_F32)
    merged = jax.nn.sigmoid(ga_ref[...]) * ua + jax.nn.sigmoid(gb_ref[...]) * ub
    o_ref[...] = x_ref[...] + jnp.dot(merged.astype(_BF), wo_ref[...], preferred_element_type=_F32)


def _merge(oa, ob, ga, gb, x, wa, wb, wo, tm):
    M, D = x.shape
    W = oa.shape[1]
    rowD = pl.BlockSpec((tm, D), lambda i: (i, 0))
    rowW = pl.BlockSpec((tm, W), lambda i: (i, 0))
    const = lambda shape: pl.BlockSpec(shape, lambda i: (0, 0), pipeline_mode=pl.Buffered(1))
    return pl.pallas_call(
        _merge_kernel,
        grid=(M // tm,),
        in_specs=[rowW, rowW, rowD, rowD, rowD, const((W, D)), const((W, D)), const((D, D))],
        out_specs=rowD,
        out_shape=jax.ShapeDtypeStruct((M, D), _F32),
        compiler_params=_cparams(("arbitrary",)),
        name="merge",
    )(oa, ob, ga, gb, x, wa.astype(_BF), wb.astype(_BF), wo.astype(_BF))


def _ffn_gate_down(y, x, wd_ref, dff):
    a = y[:, :dff]
    g = y[:, dff:]
    act = (g * jax.nn.sigmoid(g)) * a
    return x + jnp.dot(act.astype(_BF), wd_ref[...], preferred_element_type=_F32)


def _ffn_prompt_kernel(x_ref, g_ref, wu_ref, cw_ref, cb_ref, wd_ref, o_ref, st_ref, u_ref, *, tm, dff):
    @pl.when(pl.program_id(0) == 0)
    def _():
        u_ref[0:8, :] = jnp.zeros((8, 2 * dff), _F32)

    x = x_ref[...]
    h = _rms(x, g_ref[...]).astype(_BF)
    u_ref[8:8 + tm, :] = jnp.dot(h, wu_ref[...], preferred_element_type=_F32)
    y = (cb_ref[...] + cw_ref[0:1, :] * u_ref[6:6 + tm, :] + cw_ref[1:2, :] * u_ref[7:7 + tm, :]
         + cw_ref[2:3, :] * u_ref[8:8 + tm, :])
    o_ref[...] = _ffn_gate_down(y, x, wd_ref, dff)
    tail = u_ref[8 + tm - 2:8 + tm, :]
    st_ref[...] = tail
    u_ref[6:8, :] = tail


def _ffn_prompt(x, gamma, w_up, conv_w, conv_b, w_down, tm=256):
    T, D = x.shape
    dff = w_down.shape[0]
    const = lambda shape: pl.BlockSpec(shape, lambda i: (0, 0), pipeline_mode=pl.Buffered(1))
    return pl.pallas_call(
        functools.partial(_ffn_prompt_kernel, tm=tm, dff=dff),
        grid=(T // tm,),
        in_specs=[pl.BlockSpec((tm, D), lambda i: (i, 0)), const((1, D)), const((D, 2 * dff)),
                  const((3, 2 * dff)), const((1, 2 * dff)), const((dff, D))],
        out_specs=[pl.BlockSpec((tm, D), lambda i: (i, 0)), pl.BlockSpec((2, 2 * dff), lambda i: (0, 0))],
        out_shape=[jax.ShapeDtypeStruct((T, D), _F32), jax.ShapeDtypeStruct((2, 2 * dff), _F32)],
        scratch_shapes=[pltpu.VMEM((tm + 8, 2 * dff), _F32)],
        compiler_params=_cparams(("arbitrary",)),
        name="ffn_prompt",
    )(x, gamma.reshape(1, D), w_up.astype(_BF), conv_w, conv_b.reshape(1, -1), w_down.astype(_BF))


def _ffn_sample_kernel(x_ref, g_ref, wu_ref, cw_ref, cb_ref, wd_ref, s0_ref, s1_ref, o_ref, u_out_ref, *, dff):
    x = x_ref[...]
    h = _rms(x, g_ref[...]).astype(_BF)
    u = jnp.dot(h, wu_ref[...], preferred_element_type=_F32)
    y = cb_ref[...] + cw_ref[0:1, :] * s0_ref[...] + cw_ref[1:2, :] * s1_ref[...] + cw_ref[2:3, :] * u
    o_ref[...] = _ffn_gate_down(y, x, wd_ref, dff)
    u_out_ref[...] = u


def _ffn_sample(x, gamma, w_up, conv_w, conv_b, w_down, state):
    B, D = x.shape
    dff = w_down.shape[0]
    return pl.pallas_call(
        functools.partial(_ffn_sample_kernel, dff=dff),
        out_shape=[jax.ShapeDtypeStruct((B, D), _F32), jax.ShapeDtypeStruct((B, 2 * dff), _F32)],
        compiler_params=pltpu.CompilerParams(vmem_limit_bytes=VMEM_LIMIT),
        name="ffn_sample",
    )(x, gamma.reshape(1, D), w_up.astype(_BF), conv_w, conv_b.reshape(1, -1), w_down.astype(_BF),
      state[:, 0, :], state[:, 1, :])


def _norm_kernel(x_ref, g_ref, o_ref):
    o_ref[...] = _rms(x_ref[...], g_ref[...])


def _final_norm(x, gamma, tm):
    M, D = x.shape
    return pl.pallas_call(
        _norm_kernel,
        grid=(M // tm,),
        in_specs=[pl.BlockSpec((tm, D), lambda i: (i, 0)), pl.BlockSpec((1, D), lambda i: (0, 0))],
        out_specs=pl.BlockSpec((tm, D), lambda i: (i, 0)),
        out_shape=jax.ShapeDtypeStruct((M, D), _F32),
        compiler_params=_cparams(("arbitrary",)),
        name="final_norm",
    )(x, gamma.reshape(1, D))


def _proj_sample_kernel(x_ref, g_ref, w_ref, o_ref):
    h = _rms(x_ref[...], g_ref[...]).astype(_BF)
    o_ref[...] = jnp.dot(h, w_ref[...], preferred_element_type=_F32)


def _proj_sample(x, gamma, w_in):
    B, D = x.shape
    n = w_in.shape[1]
    npad = -(-n // LANES) * LANES
    w = jnp.pad(w_in, ((0, 0), (0, npad - n))).astype(_BF)
    z = pl.pallas_call(
        _proj_sample_kernel,
        out_shape=jax.ShapeDtypeStruct((B, npad), _F32),
        compiler_params=pltpu.CompilerParams(vmem_limit_bytes=VMEM_LIMIT),
        name="proj_sample",
    )(x, gamma.reshape(1, D), w)
    return z[:, :n]


def _logsig_kernel(f_ref, b_ref, o_ref):
    o_ref[...] = _log_sigmoid(f_ref[...] + b_ref[...])


def _log_sigmoid_rows(f, b):
    return pl.pallas_call(
        _logsig_kernel,
        out_shape=jax.ShapeDtypeStruct(f.shape, _F32),
        name="sample_logf",
    )(f, b.reshape(1, -1))


def _pages_last(cache):
    nd = cache.ndim
    return jnp.transpose(cache, (0, 1) + tuple(range(3, nd)) + (2,))


SCORE_GROUP = 8


def _sample_scores_kernel(pt_sm, pool_ref, q8_ref, w8_ref, knew_ref, o_ref, onew_ref, buf, sem,
                          *, n_pages, n_batch, layer):
    b = pl.program_id(0)

    def page_copy(bb, p, slot):
        return pltpu.make_async_copy(pool_ref.at[layer, pt_sm[bb, p]], buf.at[slot, p], sem.at[slot])

    def start_all(bb, slot):
        def body(p, c):
            page_copy(bb, p, slot).start()
            return c
        lax.fori_loop(0, n_pages, body, 0)

    @pl.when(b == 0)
    def _():
        start_all(0, 0)

    @pl.when(b + 1 < n_batch)
    def _():
        start_all(b + 1, (b + 1) % 2)

    slot = b % 2

    def wait_body(p, c):
        page_copy(b, p, slot).wait()
        return c
    lax.fori_loop(0, n_pages, wait_body, 0)

    q8 = q8_ref[0]
    w8 = w8_ref[0]
    gsz = SCORE_GROUP
    for g in range(n_pages // gsz):
        keys = buf[slot, g * gsz:(g + 1) * gsz].reshape(gsz * IDX_DIM, PAGE_SIZE).astype(_BF)
        s = jnp.maximum(jnp.dot(q8, keys, preferred_element_type=_F32), 0.0) * w8
        o_ref[0, g * gsz:(g + 1) * gsz, :] = jnp.sum(s.reshape(gsz, IDX_HEADS, PAGE_SIZE), axis=1)
    q = q8[0:IDX_HEADS, 0:IDX_DIM]
    sn = _dot_nt(q, knew_ref[0].astype(_BF))
    onew_ref[0] = jnp.sum(jnp.maximum(sn, 0.0) * w8[0:IDX_HEADS], axis=0, keepdims=True)


def _sample_scores(cache_kidx, layer, page_table, q_idx, w_idx, k_new):
    B, NP = page_table.shape
    gsz = SCORE_GROUP
    assert NP % gsz == 0
    pool = _pages_last(cache_kidx)
    idx_scale = (IDX_DIM ** -0.5) * (IDX_HEADS ** -0.5)
    eye = jnp.eye(gsz, dtype=q_idx.dtype)
    q8 = (eye[None, :, None, :, None] * q_idx[:, None, :, None, :]).reshape(B, gsz * IDX_HEADS, gsz * IDX_DIM)
    w8 = jnp.tile(w_idx * idx_scale, (1, gsz))[:, :, None]
    knew8 = jnp.pad(k_new[:, None, :], ((0, 0), (0, 7), (0, 0)))
    per_b = lambda b, pt: (b, 0, 0)
    gs = pltpu.PrefetchScalarGridSpec(
        num_scalar_prefetch=1,
        grid=(B,),
        in_specs=[pl.BlockSpec(memory_space=pl.ANY),
                  pl.BlockSpec((1, gsz * IDX_HEADS, gsz * IDX_DIM), per_b),
                  pl.BlockSpec((1, gsz * IDX_HEADS, 1), per_b),
                  pl.BlockSpec((1, 8, IDX_DIM), per_b)],
        out_specs=[pl.BlockSpec((1, NP, PAGE_SIZE), per_b), pl.BlockSpec((1, 1, 8), per_b)],
        scratch_shapes=[pltpu.VMEM((2, NP, IDX_DIM, PAGE_SIZE), _F32), pltpu.SemaphoreType.DMA((2,))],
    )
    sc, sn = pl.pallas_call(
        functools.partial(_sample_scores_kernel, n_pages=NP, n_batch=B, layer=layer),
        grid_spec=gs,
        out_shape=[jax.ShapeDtypeStruct((B, NP, PAGE_SIZE), _F32), jax.ShapeDtypeStruct((B, 1, 8), _F32)],
        compiler_params=_cparams(("arbitrary",)),
        name="sample_scores",
    )(page_table, pool, q8.astype(_BF), w8, knew8)
    return sc, sn[:, 0, 0]


def _topk_kernel(x_ref, o_ref, buf_ref, *, k):
    B, L = x_ref.shape
    buf_ref[...] = x_ref[...]
    lane = lax.broadcasted_iota(jnp.int32, (B, L), 1).astype(_F32)

    def body(r, c):
        x = buf_ref[...]
        m = jnp.max(x, axis=1, keepdims=True)
        idx = jnp.min(jnp.where(x == m, lane, float(L)), axis=1, keepdims=True)
        buf_ref[...] = jnp.where(lane == idx, -jnp.inf, x)
        return c

    lax.fori_loop(0, k, body, 0)
    o_ref[...] = jnp.where(buf_ref[...] != x_ref[...], 1.0, 0.0)


def _topk_mask(scores, k):
    B, L = scores.shape
    return pl.pallas_call(
        functools.partial(_topk_kernel, k=k),
        out_shape=jax.ShapeDtypeStruct((B, L), _F32),
        scratch_shapes=[pltpu.VMEM((B, L), _F32)],
        compiler_params=pltpu.CompilerParams(vmem_limit_bytes=VMEM_LIMIT),
        name="sample_topk",
    )(scores)


def _bucket_edges():
    tab = _bucket_table(4 * MAX_DIST)
    max_exact = N_BUCKETS // 2
    edges = [int(np.argmax(tab >= b)) for b in range(max_exact + 1, N_BUCKETS)]
    for b, e in zip(range(max_exact + 1, N_BUCKETS), edges):
        assert tab[e] == b and tab[e - 1] == b - 1
    return edges


def _bias_rows(dist, rbT):
    max_exact = N_BUCKETS // 2
    bucket = jnp.full(dist.shape, max_exact, jnp.int32)
    for e in _bucket_edges():
        bucket = bucket + jnp.where(dist >= e, 1, 0)
    bucket = jnp.where(dist < max_exact, dist, bucket)
    onehot = (lax.broadcasted_iota(jnp.int32, (N_BUCKETS, dist.shape[1]), 0) == bucket).astype(_BF)
    return _dot3(rbT, onehot)


def _head_block_mask():
    r = lax.broadcasted_iota(jnp.int32, (N_HEADS, N_HEADS * HEAD_DIM), 0)
    c = lax.broadcasted_iota(jnp.int32, (N_HEADS, N_HEADS * HEAD_DIM), 1)
    return (c // HEAD_DIM) == r


def _paged_attn_kernel(pt_sm, kc_ref, vc_ref, *rest, forget, n_groups, group, n_batch, layer, past):
    if forget:
        (lf_ref, qbd_ref, knew_ref, vnew_ref, lfnew_ref, upper_ref,
         o_ref, kbuf, vbuf, lfbuf, sem, m_ref, l_ref, acc_ref, c_ref) = rest
    else:
        (sel_ref, selnew_ref, qbd_ref, knew_ref, vnew_ref, rbT_ref,
         o_ref, kbuf, vbuf, sem, m_ref, l_ref, acc_ref, bias_ref) = rest
    b = pl.program_id(0)
    gi = pl.program_id(1)
    n = b * n_groups + gi
    W = N_HEADS * HEAD_DIM

    def copies(bb, gg, slot):
        out = []
        for g in range(group):
            page = pt_sm[bb, gg * group + g]
            out.append(pltpu.make_async_copy(kc_ref.at[layer, page], kbuf.at[slot, g], sem.at[slot]))
            out.append(pltpu.make_async_copy(vc_ref.at[layer, page], vbuf.at[slot, g], sem.at[slot]))
            if forget:
                out.append(pltpu.make_async_copy(lf_ref.at[layer, page], lfbuf.at[slot, g], sem.at[slot]))
        return out

    @pl.when(n == 0)
    def _():
        for cp in copies(0, 0, 0):
            cp.start()

    @pl.when(n + 1 < n_batch * n_groups)
    def _():
        nxt = n + 1
        for cp in copies(nxt // n_groups, nxt % n_groups, nxt % 2):
            cp.start()

    slot = n % 2
    for cp in copies(b, gi, slot):
        cp.wait()

    @pl.when(gi == 0)
    def _():
        m_ref[...] = jnp.full(m_ref.shape, NEG, _F32)
        l_ref[...] = jnp.zeros(l_ref.shape, _F32)
        acc_ref[...] = jnp.zeros(acc_ref.shape, _F32)
        if forget:
            c_ref[...] = jnp.zeros(c_ref.shape, _F32)
        else:
            bias_ref[...] = jnp.broadcast_to(rbT_ref[:, N_BUCKETS - 1:N_BUCKETS], bias_ref.shape)

    if not forget:
        @pl.when(gi == n_groups - 1)
        def _():
            lane = lax.broadcasted_iota(jnp.int32, (1, group * PAGE_SIZE), 1)
            dist = past - (gi * group * PAGE_SIZE + lane)
            bias_ref[...] = _bias_rows(dist, rbT_ref[...])

    qbd = qbd_ref[0]
    if forget:
        upper = upper_ref[...]
        carry = c_ref[...]
    pieces = []
    for g in range(group):
        keys = kbuf[slot, g].reshape(W, PAGE_SIZE).astype(_BF)
        s = jnp.dot(qbd, keys, preferred_element_type=_F32)
        if forget:
            csum = carry + _dot3(lfbuf[slot, g], upper)
            carry = csum[:, PAGE_SIZE - 1:PAGE_SIZE]
            s = s - csum
        pieces.append(s)
    s = jnp.concatenate(pieces, axis=1)
    if forget:
        c_ref[...] = carry
    else:
        sel = sel_ref[0].reshape(1, group * PAGE_SIZE)
        s = jnp.where(sel > 0.0, s + bias_ref[...], NEG)
    m_old = m_ref[...]
    m_new = jnp.maximum(m_old, jnp.max(s, axis=1, keepdims=True))
    alpha = jnp.exp(m_old - m_new)
    pr = jnp.exp(s - m_new).astype(_BF)
    l_ref[...] = alpha * l_ref[...] + jnp.sum(pr.astype(_F32), axis=1, keepdims=True)
    pv = jnp.zeros((N_HEADS, W), _F32)
    for g in range(group):
        vals = vbuf[slot, g].reshape(W, PAGE_SIZE).astype(_BF)
        pv = pv + _dot_nt(pr[:, g * PAGE_SIZE:(g + 1) * PAGE_SIZE], vals)
    acc_ref[...] = alpha * acc_ref[...] + pv
    m_ref[...] = m_new

    @pl.when(gi == n_groups - 1)
    def _():
        s_new = jnp.sum(qbd.astype(_F32) * knew_ref[0], axis=1, keepdims=True)
        m_o = m_ref[...]
        if forget:
            s_new = s_new - (c_ref[...] + lfnew_ref[0])
            m_n = jnp.maximum(m_o, s_new)
            p_new = jnp.exp(s_new - m_n)
        else:
            picked = selnew_ref[0][:, 0:1] > 0.0
            s_new = s_new + rbT_ref[:, 0:1]
            m_n = jnp.where(picked, jnp.maximum(m_o, s_new), m_o)
            p_new = jnp.where(picked, jnp.exp(s_new - m_n), 0.0)
        a = jnp.exp(m_o - m_n)
        l_fin = a * l_ref[...] + p_new
        full = (a * acc_ref[...] + p_new * vnew_ref[0]) / l_fin
        o_ref[0] = jnp.sum(jnp.where(_head_block_mask(), full, 0.0), axis=0, keepdims=True)


def _block_diag_q(q):
    B = q.shape[0]
    qh = (q * (HEAD_DIM ** -0.5)).reshape(B, N_HEADS, 1, HEAD_DIM)
    eye = jnp.eye(N_HEADS, dtype=q.dtype)[None, :, :, None]
    return (qh * eye).reshape(B, N_HEADS, N_HEADS * HEAD_DIM).astype(_BF)


def _paged_attention(cache_k, cache_v, layer, page_table, q, k_new, v_new, *, logf=None, logf_new=None,
                     sel=None, sel_new=None, rel_bias=None, group=16):
    forget = logf is not None
    B, NP = page_table.shape
    group = min(group, NP)
    assert NP % group == 0
    n_groups = NP // group
    W = N_HEADS * HEAD_DIM
    kT = _pages_last(cache_k)
    vT = _pages_last(cache_v)
    any_spec = pl.BlockSpec(memory_space=pl.ANY)
    per_b = lambda b, g, pt: (b, 0, 0)
    common_in = [pl.BlockSpec((1, N_HEADS, W), per_b), pl.BlockSpec((1, 1, W), per_b), pl.BlockSpec((1, 1, W), per_b)]
    common_args = (_block_diag_q(q), k_new[:, None, :], v_new[:, None, :])
    page_buf = pltpu.VMEM((2, group, N_HEADS, HEAD_DIM, PAGE_SIZE), _F32)
    stats = [pltpu.VMEM((N_HEADS, 1), _F32), pltpu.VMEM((N_HEADS, 1), _F32), pltpu.VMEM((N_HEADS, W), _F32)]
    if forget:
        lane = np.arange(PAGE_SIZE)
        upper = jnp.asarray(lane[:, None] <= lane[None, :], _BF)
        in_specs = [any_spec, any_spec, any_spec] + common_in + [
            pl.BlockSpec((1, N_HEADS, 1), per_b), pl.BlockSpec((PAGE_SIZE, PAGE_SIZE), lambda b, g, pt: (0, 0))]
        args = (kT, vT, _pages_last(logf)) + common_args + (logf_new[:, :, None], upper)
        scratch = [page_buf, page_buf, pltpu.VMEM((2, group, N_HEADS, PAGE_SIZE), _F32),
                   pltpu.SemaphoreType.DMA((2,))] + stats + [pltpu.VMEM((N_HEADS, 1), _F32)]
    else:
        in_specs = [any_spec, any_spec,
                    pl.BlockSpec((1, group, PAGE_SIZE), lambda b, g, pt: (b, g, 0)),
                    pl.BlockSpec((1, 1, PAGE_SIZE), per_b)] + common_in + [
            pl.BlockSpec((N_HEADS, N_BUCKETS), lambda b, g, pt: (0, 0))]
        args = (kT, vT, sel, sel_new) + common_args + (rel_bias.T,)
        scratch = [page_buf, page_buf, pltpu.SemaphoreType.DMA((2,))] + stats + [
            pltpu.VMEM((N_HEADS, group * PAGE_SIZE), _F32)]
    gs = pltpu.PrefetchScalarGridSpec(
        num_scalar_prefetch=1,
        grid=(B, n_groups),
        in_specs=in_specs,
        out_specs=pl.BlockSpec((1, 1, W), per_b),
        scratch_shapes=scratch,
    )
    out = pl.pallas_call(
        functools.partial(_paged_attn_kernel, forget=forget, n_groups=n_groups, group=group, n_batch=B,
                          layer=layer, past=NP * PAGE_SIZE),
        grid_spec=gs,
        out_shape=jax.ShapeDtypeStruct((B, 1, W), _F32),
        compiler_params=_cparams(("arbitrary", "arbitrary")),
        name="fox_sample" if forget else "dsa_sample",
    )(page_table, *args)
    return out[:, 0, :]


def _prompt_attention_inputs(pp):
    T = pp["k_a"].shape[0]
    heads = lambda x: x.reshape(T, N_HEADS, HEAD_DIM).transpose(1, 0, 2).astype(_BF)
    pad_k = lambda k, cols: jnp.concatenate(
        [k] + cols + [jnp.zeros((N_HEADS, T, LANES - HEAD_DIM - len(cols)), _BF)], axis=-1)
    pad_q = lambda qT, n1: jnp.concatenate(
        [qT.reshape(N_HEADS, HEAD_DIM, T), jnp.ones((N_HEADS, n1, T), _BF),
         jnp.zeros((N_HEADS, LANES - HEAD_DIM - n1, T), _BF)], axis=1)
    c = _cumsum_time(pp["logfT"])
    cols = [x[:, :, None] for x in _split3(-LOG2E * c)]
    fox = (pad_k(heads(pp["k_b"]), cols), _values_with_ones(pp["v_bT"]), pad_q(pp["q_bT"], 3))
    dsa = (pad_k(heads(pp["k_a"]), []), _values_with_ones(pp["v_aT"]), pad_q(pp["q_aT"], 0))
    return fox, dsa


def kernel(x_prompt, x_sample, cache_k_a, cache_v_a, cache_kidx, cache_k_b, cache_v_b, cache_logf_b,
           state_conv, page_table, rel_bias, attn_norm, w_in, b_forget, w_a_up, w_b_up, w_o,
           ffn_norm, w_up, conv_w, conv_b, w_down, final_norm):
    depth = w_in.shape[0]
    Bp, T, D = x_prompt.shape
    Bs = x_sample.shape[0]
    assert Bp == 1 and x_sample.shape[1] == 1
    NP = page_table.shape[1]
    P = NP * PAGE_SIZE
    W = N_HEADS * HEAD_DIM
    topk_s = min(TOPK_MAX, (P + 1) // 4)
    tm = min(512, T)

    xp = x_prompt[0]
    xs = x_sample[:, 0, :]
    sizes = (W, W, W, IDX_HEADS * IDX_DIM, IDX_DIM, IDX_HEADS, W, W, W, N_HEADS, D, D)
    cuts = np.cumsum((0,) + sizes)
    new = [[] for _ in range(14)]
    for l in range(depth):
        pp = _proj_prompt(xp, attn_norm[l], w_in[l], b_forget[l])
        fox, dsa = _prompt_attention_inputs(pp)
        obT = _fox_prompt(*fox)
        oaT = _dsa_prompt(pp["k_i"][:, :IDX_DIM].astype(_BF), pp["q_iT"], pp["w_iT"], *dsa, rel_bias)
        xp = _merge(oaT.T, obT.T, pp["g_a"], pp["g_b"], xp, w_a_up[l], w_b_up[l], w_o[l], tm)
        xp, conv_p = _ffn_prompt(xp, ffn_norm[l], w_up[l], conv_w[l], conv_b[l], w_down[l])

        z = _proj_sample(xs, attn_norm[l], w_in[l])
        (q_a, k_a, v_a, q_i, k_i, w_i, q_b, k_b, v_b, f_b, g_a, g_b) = [z[:, cuts[k]:cuts[k + 1]] for k in range(12)]
        logf_s = _log_sigmoid_rows(f_b, b_forget[l])
        sc, sn = _sample_scores(cache_kidx, l, page_table, q_i.reshape(Bs, IDX_HEADS, IDX_DIM), w_i, k_i)
        pad = jnp.full((Bs, PAGE_SIZE - 1), -jnp.inf, _F32)
        scores = jnp.concatenate([sc.reshape(Bs, P), sn[:, None], pad], axis=1)
        sel = _topk_mask(scores, topk_s).reshape(Bs, NP + 1, PAGE_SIZE)
        oa_s = _paged_attention(cache_k_a, cache_v_a, l, page_table, q_a, k_a, v_a,
                                sel=sel[:, :NP], sel_new=sel[:, NP:], rel_bias=rel_bias)
        ob_s = _paged_attention(cache_k_b, cache_v_b, l, page_table, q_b, k_b, v_b,
                                logf=cache_logf_b, logf_new=logf_s)
        xs = _merge(oa_s.astype(_BF), ob_s.astype(_BF), g_a, g_b, xs, w_a_up[l], w_b_up[l], w_o[l], Bs)
        xs, u_s = _ffn_sample(xs, ffn_norm[l], w_up[l], conv_w[l], conv_b[l], w_down[l], state_conv[l])
        conv_s = jnp.stack([state_conv[l][:, 1, :], u_s], axis=1)

        hd = lambda t, n: t.reshape(t.shape[0], n, HEAD_DIM)
        rows = (hd(pp["k_a"], N_HEADS)[None], hd(k_a, N_HEADS)[:, None],
                hd(pp["v_a"], N_HEADS)[None], hd(v_a, N_HEADS)[:, None],
                pp["k_i"][None, :, :IDX_DIM], k_i[:, None],
                hd(pp["k_b"], N_HEADS)[None], hd(k_b, N_HEADS)[:, None],
                hd(pp["v_b"], N_HEADS)[None], hd(v_b, N_HEADS)[:, None],
                pp["logfT"].T[None], logf_s[:, None],
                conv_p[None], conv_s)
        for lst, r in zip(new, rows):
            lst.append(r)
    y_prompt = _final_norm(xp, final_norm, tm)[None]
    y_sample = _final_norm(xs, final_norm, Bs)[:, None]
    return (y_prompt, y_sample) + tuple(jnp.stack(n) for n in new)
```

```python
import functools
import math

import numpy as np
import jax
import jax.numpy as jnp
from jax import lax
from jax.experimental import pallas as pl
from jax.experimental.pallas import tpu as pltpu

HEAD_DIM = 64
N_HEADS = 8
IDX_HEADS = 8
IDX_DIM = 32
TOPK_MAX = 256
N_BUCKETS = 32
MAX_DIST = 128
PAGE_SIZE = 128
NORM_EPS = 1e-6
NEG = -1e30
INT_MIN = -2 ** 31
LANES = 128
V_ROWS = 80
VMEM_LIMIT = 56 * 1024 * 1024
LOG2E = math.log2(math.e)

_BF = jnp.bfloat16
_F32 = jnp.float32


def _cparams(sem):
    return pltpu.CompilerParams(dimension_semantics=sem, vmem_limit_bytes=VMEM_LIMIT)


def _split3(x):
    hi = x.astype(_BF)
    r1 = x - hi.astype(_F32)
    mid = r1.astype(_BF)
    lo = (r1 - mid.astype(_F32)).astype(_BF)
    return hi, mid, lo


def _dot3(x, b):
    hi, mid, lo = _split3(x)
    d = lambda a: jnp.dot(a, b, preferred_element_type=_F32)
    return d(hi) + d(mid) + d(lo)


def _rms(x, g):
    var = jnp.mean(x * x, axis=-1, keepdims=True)
    return x * lax.rsqrt(var + NORM_EPS) * g


def _log_sigmoid(x):
    return -(jnp.maximum(-x, 0.0) + jnp.log1p(jnp.exp(-jnp.abs(x))))


def _dot_nt(a, b):
    return lax.dot_general(a, b, (((1,), (1,)), ((), ())), preferred_element_type=_F32)


def _bdot(a, b):
    return lax.dot_general(a, b, (((2,), (1,)), ((0,), (0,))), preferred_element_type=_F32)


_NAT = (("k_a", 512), ("v_a", 512), ("k_b", 512), ("v_b", 512), ("g_a", 1024), ("g_b", 1024), ("k_i", 128))
_TRN = (("q_a", 512), ("q_b", 512), ("v_a", 512), ("v_b", 512), ("q_i", 256), ("w_i", 16), ("f_b", 16))


def _proj_prompt_kernel(x_ref, g_ref, wn_ref, wt_ref, bf_ref,
                        ka_ref, va_ref, kb_ref, vb_ref, ga_ref, gb_ref, ki_ref,
                        qaT_ref, qbT_ref, vaT_ref, vbT_ref, qiT_ref, wiT_ref, lfT_ref):
    h = _rms(x_ref[...], g_ref[...]).astype(_BF)
    nat_refs = (ka_ref, va_ref, kb_ref, vb_ref, ga_ref, gb_ref, ki_ref)
    off = 0
    for (name, n), ref in zip(_NAT, nat_refs):
        ref[...] = jnp.dot(h, wn_ref[:, off:off + n], preferred_element_type=_F32)
        off += n
    trn_refs = (qaT_ref, qbT_ref, vaT_ref, vbT_ref, qiT_ref, wiT_ref, lfT_ref)
    off = 0
    for (name, n), ref in zip(_TRN, trn_refs):
        z = _dot_nt(wt_ref[off:off + n, :], h)[:ref.shape[0]]
        if name == "f_b":
            z = _log_sigmoid(z + bf_ref[...])
        ref[...] = z.astype(ref.dtype)
        off += n


def _proj_prompt(x, gamma, w_in, b_f, tm=256):
    T, D = x.shape
    sizes = (512, 512, 512, 256, 32, 8, 512, 512, 512, 8, D, D)
    cuts = np.cumsum((0,) + sizes)
    col = lambda k: w_in[:, cuts[k]:cuts[k + 1]]
    q_a, k_a, v_a, q_i, k_i, w_i, q_b, k_b, v_b, f_b, g_a, g_b = [col(k) for k in range(12)]
    scale = (HEAD_DIM ** -0.5) * LOG2E
    idx_scale = (IDX_DIM ** -0.5) * (IDX_HEADS ** -0.5)
    k_i_pad = jnp.pad(k_i, ((0, 0), (0, LANES - IDX_DIM)))
    wn = jnp.concatenate([k_a, v_a, k_b, v_b, g_a, g_b, k_i_pad], axis=1).astype(_BF)
    pad8 = lambda w: jnp.pad(w, ((0, 0), (0, 8)))
    wt = jnp.concatenate([q_a * scale, q_b * scale, v_a, v_b, q_i, pad8(w_i * idx_scale), pad8(f_b)],
                         axis=1).T.astype(_BF)
    nn, nt = wn.shape[1], wt.shape[0]
    grid = (T // tm,)
    row = lambda n, dt=_F32: (jax.ShapeDtypeStruct((T, n), dt), pl.BlockSpec((tm, n), lambda i: (i, 0)))
    colT = lambda n, dt: (jax.ShapeDtypeStruct((n, T), dt), pl.BlockSpec((n, tm), lambda i: (0, i)))
    outs = [row(512), row(512), row(512), row(512), row(D), row(D), row(LANES),
            colT(512, _BF), colT(512, _BF), colT(512, _BF), colT(512, _BF), colT(256, _BF),
            colT(8, _F32), colT(8, _F32)]
    const = lambda shape: pl.BlockSpec(shape, lambda i: (0,) * len(shape), pipeline_mode=pl.Buffered(1))
    res = pl.pallas_call(
        _proj_prompt_kernel,
        grid=grid,
        in_specs=[pl.BlockSpec((tm, D), lambda i: (i, 0)), const((1, D)), const((D, nn)), const((nt, D)),
                  const((8, 1))],
        out_specs=[o[1] for o in outs],
        out_shape=[o[0] for o in outs],
        compiler_params=_cparams(("arbitrary",)),
        name="proj_prompt",
    )(x, gamma.reshape(1, D), wn, wt, b_f.reshape(8, 1))
    names = ("k_a", "v_a", "k_b", "v_b", "g_a", "g_b", "k_i",
             "q_aT", "q_bT", "v_aT", "v_bT", "q_iT", "w_iT", "logfT")
    return dict(zip(names, res))


def _cumsum_kernel(x_ref, lower_ref, o_ref):
    x = x_ref[...]
    n = x.shape[0]
    r = lax.broadcasted_iota(jnp.int32, (LANES, LANES), 0)
    c = lax.broadcasted_iota(jnp.int32, (LANES, LANES), 1)
    upper = (r <= c).astype(_BF)
    within = _dot3(x, upper)
    totals = within[:, LANES - 1:LANES]
    hi, mid, lo = _split3(totals)
    d = lambda a: jnp.dot(lower_ref[...], jnp.broadcast_to(a, (n, LANES)), preferred_element_type=_F32)
    o_ref[...] = within + (d(hi) + d(mid) + d(lo))


def _cumsum_time(logfT):
    H, T = logfT.shape
    C = T // LANES
    x = logfT.reshape(H * C, LANES)
    rows = np.arange(H * C)
    lower = (rows[:, None] // C == rows[None, :] // C) & (rows[None, :] < rows[:, None])
    out = pl.pallas_call(
        _cumsum_kernel,
        out_shape=jax.ShapeDtypeStruct((H * C, LANES), _F32),
        compiler_params=pltpu.CompilerParams(vmem_limit_bytes=VMEM_LIMIT),
        name="logf_cumsum",
    )(x, jnp.asarray(lower, _BF))
    return out.reshape(H, T)


def _attn_init(m_ref, acc_ref):
    m_ref[...] = jnp.full(m_ref.shape, NEG, _F32)
    acc_ref[...] = jnp.zeros(acc_ref.shape, _F32)


def _attn_tile(k_ref, qT_ref, vT_ref, m_ref, acc_ref, keep=None, bias=None):
    s = _bdot(k_ref[...], qT_ref[...])
    if bias is not None:
        s = s + bias
    if keep is not None:
        s = jnp.where(keep[None], s, NEG)
    m_old = m_ref[:, 0:1, :]
    m_new = jnp.maximum(m_old, jnp.max(s, axis=1, keepdims=True))
    alpha = jnp.exp2(m_old - m_new)
    p = jnp.exp2(s - m_new).astype(_BF)
    acc_ref[...] = alpha * acc_ref[...] + _bdot(vT_ref[...], p)
    m_ref[...] = jnp.broadcast_to(m_new, m_ref.shape)


def _attn_finish(o_ref, acc_ref):
    out = acc_ref[:, :HEAD_DIM, :] / acc_ref[:, HEAD_DIM:HEAD_DIM + 1, :]
    o_ref[...] = out.reshape(o_ref.shape).astype(o_ref.dtype)


def _values_with_ones(vT):
    T = vT.shape[1]
    v = vT.reshape(N_HEADS, HEAD_DIM, T)
    ones = jnp.ones((N_HEADS, 1, T), _BF)
    zeros = jnp.zeros((N_HEADS, V_ROWS - HEAD_DIM - 1, T), _BF)
    return jnp.concatenate([v, ones, zeros], axis=1)


def _fox_kernel(it_ref, jt_ref, skip_ref, jf_ref, k_ref, vT_ref, qT_ref, o_ref, m_ref, acc_ref, *, tq, tk):
    step = pl.program_id(0)
    i = it_ref[step]
    j = jt_ref[step]
    last_j = ((i + 1) * tq - 1) // tk

    @pl.when(j == 0)
    def _():
        _attn_init(m_ref, acc_ref)

    needs_mask = (j + 1) * tk - 1 > i * tq
    active = skip_ref[step] == 0

    @pl.when(needs_mask)
    def _():
        srow = j * tk + lax.broadcasted_iota(jnp.int32, (tk, tq), 0)
        tcol = i * tq + lax.broadcasted_iota(jnp.int32, (tk, tq), 1)
        _attn_tile(k_ref, qT_ref, vT_ref, m_ref, acc_ref, keep=srow <= tcol)

    @pl.when(jnp.logical_not(needs_mask) & active)
    def _():
        _attn_tile(k_ref, qT_ref, vT_ref, m_ref, acc_ref)

    @pl.when(j == last_j)
    def _():
        _attn_finish(o_ref, acc_ref)


def _causal_steps(nq, tq, tk):
    ii, jj = [], []
    for i in range(nq):
        for j in range(((i + 1) * tq - 1) // tk + 1):
            ii.append(i)
            jj.append(j)
    return np.asarray(ii, np.int32), np.asarray(jj, np.int32)


SKIP_MARGIN = 168.0


def _fox_skip_table(k_aug, qT_aug, it, jt, tq, tk):
    f32 = lambda x: x.astype(_F32)
    kk, ee = f32(k_aug[:, :, :HEAD_DIM]), jnp.sum(f32(k_aug[:, :, HEAD_DIM:HEAD_DIM + 3]), axis=-1)
    kn = jnp.sqrt(jnp.sum(kk * kk, axis=-1)) * 1.01
    qq = f32(qT_aug[:, :HEAD_DIM, :])
    qn = jnp.sqrt(jnp.sum(qq * qq, axis=1)) * 1.01
    H, T = kn.shape
    blk = lambda x, n, red: red(x.reshape(H, T // n, n), axis=-1)
    upper = blk(ee, tk, jnp.max)[:, :, None] + blk(kn, tk, jnp.max)[:, :, None] * blk(qn, tq, jnp.max)[:, None, :]
    lower = blk(ee - qn * kn, tq, jnp.min)[:, None, :]
    skip = jnp.all(upper + SKIP_MARGIN < lower, axis=0)
    diag = (jt + 1) * tk - 1 > it * tq
    skip_steps = jnp.where(jnp.asarray(diag), False, skip[jt, it])
    n = len(it)
    idx = jnp.where(skip_steps, n, jnp.arange(n))
    nxt = lax.cummin(idx[::-1])[::-1]
    jfetch = jnp.asarray(jt)[jnp.minimum(nxt, n - 1)]
    return skip_steps.astype(jnp.int32), jfetch.astype(jnp.int32)


def _fox_prompt(k_aug, vT_aug, qT_aug, tq=512, tk=512):
    T = k_aug.shape[1]
    tq, tk = min(tq, T), min(tk, T)
    it, jt = _causal_steps(T // tq, tq, tk)
    skip, jfetch = _fox_skip_table(k_aug, qT_aug, it, jt, tq, tk)
    gs = pltpu.PrefetchScalarGridSpec(
        num_scalar_prefetch=4,
        grid=(len(it),),
        in_specs=[pl.BlockSpec((N_HEADS, tk, LANES), lambda s, it, jt, sk, jf: (0, jf[s], 0)),
                  pl.BlockSpec((N_HEADS, V_ROWS, tk), lambda s, it, jt, sk, jf: (0, 0, jf[s])),
                  pl.BlockSpec((N_HEADS, LANES, tq), lambda s, it, jt, sk, jf: (0, 0, it[s]))],
        out_specs=pl.BlockSpec((N_HEADS * HEAD_DIM, tq), lambda s, it, jt, sk, jf: (0, it[s])),
        scratch_shapes=[pltpu.VMEM((N_HEADS, 8, tq), _F32), pltpu.VMEM((N_HEADS, V_ROWS, tq), _F32)],
    )
    return pl.pallas_call(
        functools.partial(_fox_kernel, tq=tq, tk=tk),
        grid_spec=gs,
        out_shape=jax.ShapeDtypeStruct((N_HEADS * HEAD_DIM, T), _BF),
        compiler_params=_cparams(("arbitrary",)),
        name="fox_prompt",
    )(jnp.asarray(it), jnp.asarray(jt), skip, jfetch, k_aug, vT_aug, qT_aug)


def _sort_key(x):
    b = lax.bitcast_convert_type(x, jnp.int32)
    return b ^ ((b >> 31) & jnp.int32(0x7FFFFFFF))


def _dsa_kernel(it_ref, jt_ref, ph_ref, ki_ref, qiT_ref, wiT_ref, k_ref, vT_ref, qT_ref, toep_ref,
                o_ref, key_ref, thr_ref, m_ref, acc_ref, *, tq, tk, topk):
    step = pl.program_id(0)
    i = it_ref[step]
    j = jt_ref[step]
    phase = ph_ref[step]
    last_j = ((i + 1) * tq - 1) // tk
    chunk_rows = lambda c: pl.ds(pl.multiple_of(c * tq, tq), tq)

    @pl.when(phase == 0)
    def _():
        w = wiT_ref[...]

        def chunk(c, carry):
            kc = ki_ref[chunk_rows(c), :]
            acc = jnp.zeros((tq, tq), _F32)
            for h in range(IDX_HEADS):
                s = jnp.dot(kc, qiT_ref[h * IDX_DIM:(h + 1) * IDX_DIM, :], preferred_element_type=_F32)
                acc = acc + jnp.maximum(s, 0.0) * w[h:h + 1, :]
            key_ref[chunk_rows(c), :] = _sort_key(acc)
            return carry

        lax.fori_loop(0, i, chunk, 0)
        chunk(i, 0)
        srow = lax.broadcasted_iota(jnp.int32, (tq, tq), 0)
        tcol = lax.broadcasted_iota(jnp.int32, (tq, tq), 1)
        key_ref[chunk_rows(i), :] = jnp.where(srow <= tcol, key_ref[chunk_rows(i), :], INT_MIN)

        def pad_chunk(c, carry):
            key_ref[chunk_rows(c), :] = jnp.full((tq, tq), INT_MIN, jnp.int32)
            return carry

        lax.fori_loop(i + 1, (last_j + 1) * (tk // tq), pad_chunk, 0)

        def bit_step(b, prefix):
            bit = 31 - b
            trial_u = prefix | lax.shift_left(jnp.int32(1), bit)
            trial_s = trial_u ^ jnp.int32(INT_MIN)

            def count(c, cnt8):
                hit = jnp.where(key_ref[pl.ds(pl.multiple_of(c * tk, tk), tk), :] >= trial_s, 1, 0)
                return cnt8 + jnp.sum(hit.reshape(tk // 8, 8, tq), axis=0)

            cnt8 = lax.fori_loop(0, last_j + 1, count, jnp.zeros((8, tq), jnp.int32))
            cnt = jnp.sum(cnt8, axis=0, keepdims=True)
            return jnp.where(cnt >= topk, trial_u, prefix)

        prefix = lax.fori_loop(0, 32, bit_step, jnp.zeros((1, tq), jnp.int32))
        thr = jnp.maximum(prefix ^ jnp.int32(INT_MIN), INT_MIN + 1)
        thr_ref[...] = thr

        def block_rows(c):
            return pl.ds(pl.multiple_of(c * tk, tk), tk)

        def count_where(pred):
            def body(c, cnt8):
                hit = jnp.where(pred(c, key_ref[block_rows(c), :]), 1, 0)
                return cnt8 + jnp.sum(hit.reshape(tk // 8, 8, tq), axis=0)
            cnt8 = lax.fori_loop(0, last_j + 1, body, jnp.zeros((8, tq), jnp.int32))
            return jnp.sum(cnt8, axis=0, keepdims=True)

        excess = count_where(lambda c, blk: blk >= thr) - topk

        max_excess = jnp.max(excess)

        @pl.when(max_excess > 0)
        def _():
            row = lambda c: c * tk + lax.broadcasted_iota(jnp.int32, (tk, tq), 0)

            def highest_tied_below(cut):
                def body(c, best8):
                    r = row(c)
                    tied = (key_ref[block_rows(c), :] == thr) & (r < cut)
                    cand = jnp.where(tied, r.astype(_F32), -1.0)
                    return jnp.maximum(best8, jnp.max(cand.reshape(tk // 8, 8, tq), axis=0))
                best8 = lax.fori_loop(0, last_j + 1, body, jnp.full((8, tq), -1.0, _F32))
                return jnp.max(best8, axis=0, keepdims=True).astype(jnp.int32)

            def drop_one(r, cut):
                return jnp.where(excess > r, highest_tied_below(cut), cut)

            cut = lax.fori_loop(0, max_excess, drop_one, jnp.full((1, tq), 2 ** 30, jnp.int32))

            def demote(c, carry):
                blk = key_ref[block_rows(c), :]
                key_ref[block_rows(c), :] = jnp.where((blk == thr) & (row(c) >= cut), INT_MIN, blk)
                return carry

            lax.fori_loop(0, last_j + 1, demote, 0)

        _attn_init(m_ref, acc_ref)

    @pl.when(phase == 1)
    def _():
        sel = key_ref[pl.ds(pl.multiple_of(j * tk, tk), tk), :] >= thr_ref[...]
        delta = i * tq - j * tk
        near = delta - (tk - 1) < tq

        @pl.when(near)
        def _():
            diff = (lax.broadcasted_iota(jnp.int32, (tk, tq), 1)
                    - lax.broadcasted_iota(jnp.int32, (tk, tq), 0))
            bias_on = (diff >= -delta) & (diff < tq - delta)
            bias = jnp.where(bias_on[None], toep_ref[...], 0.0)
            _attn_tile(k_ref, qT_ref, vT_ref, m_ref, acc_ref, keep=sel, bias=bias)

        @pl.when(jnp.logical_not(near))
        def _():
            _attn_tile(k_ref, qT_ref, vT_ref, m_ref, acc_ref, keep=sel)

        @pl.when(j == last_j)
        def _():
            _attn_finish(o_ref, acc_ref)


def _bucket_table(n):
    max_exact = N_BUCKETS // 2
    d = np.arange(n)
    df = np.maximum(d.astype(np.float32), np.float32(1.0))
    large = max_exact + (np.log(df / max_exact) / math.log(MAX_DIST / max_exact)
                         * (N_BUCKETS - max_exact)).astype(np.int32)
    return np.where(d < max_exact, d, np.minimum(large, N_BUCKETS - 1)).astype(np.int32)


def _bucket_saturation():
    tab = _bucket_table(4 * MAX_DIST)
    assert tab[-1] == N_BUCKETS - 1 and np.all(np.diff(tab) >= 0)
    return int(np.argmax(tab == N_BUCKETS - 1))


def _dsa_prompt(k_i, q_iT, w_iT, k_pad, vT_aug, qT_pad, rel_bias, tq=256, tk=512):
    T = k_pad.shape[1]
    tq, tk = min(tq, T), min(tk, T)
    assert tk % tq == 0
    nq = T // tq
    topk = min(TOPK_MAX, T // 4)
    assert _bucket_saturation() <= tq
    tab = _bucket_table(tq)
    b = (rel_bias[tab, :] - rel_bias[N_BUCKETS - 1][None, :]) * LOG2E
    dmat = (np.arange(tq)[None, :] - np.arange(tq)[:, None]) % tq
    toep = jnp.tile(jnp.transpose(b[dmat], (2, 0, 1)).astype(_F32), (1, tk // tq, 1))
    it, jt, ph = [], [], []
    for i in range(nq):
        it.append(i); jt.append(0); ph.append(0)
        for j in range(((i + 1) * tq - 1) // tk + 1):
            it.append(i); jt.append(j); ph.append(1)
    it, jt, ph = (np.asarray(a, np.int32) for a in (it, jt, ph))
    const = lambda shape: pl.BlockSpec(shape, lambda s, it, jt, ph: (0,) * len(shape),
                                       pipeline_mode=pl.Buffered(1))
    gs = pltpu.PrefetchScalarGridSpec(
        num_scalar_prefetch=3,
        grid=(len(it),),
        in_specs=[const((T, IDX_DIM)),
                  pl.BlockSpec((IDX_HEADS * IDX_DIM, tq), lambda s, it, jt, ph: (0, it[s])),
                  pl.BlockSpec((IDX_HEADS, tq), lambda s, it, jt, ph: (0, it[s])),
                  pl.BlockSpec((N_HEADS, tk, LANES), lambda s, it, jt, ph: (0, jt[s], 0)),
                  pl.BlockSpec((N_HEADS, V_ROWS, tk), lambda s, it, jt, ph: (0, 0, jt[s])),
                  pl.BlockSpec((N_HEADS, LANES, tq), lambda s, it, jt, ph: (0, 0, it[s])),
                  const((N_HEADS, tk, tq))],
        out_specs=pl.BlockSpec((N_HEADS * HEAD_DIM, tq), lambda s, it, jt, ph: (0, it[s])),
        scratch_shapes=[pltpu.VMEM((T, tq), jnp.int32), pltpu.VMEM((1, tq), jnp.int32),
                        pltpu.VMEM((N_HEADS, 8, tq), _F32), pltpu.VMEM((N_HEADS, V_ROWS, tq), _F32)],
    )
    return pl.pallas_call(
        functools.partial(_dsa_kernel, tq=tq, tk=tk, topk=topk),
        grid_spec=gs,
        out_shape=jax.ShapeDtypeStruct((N_HEADS * HEAD_DIM, T), _BF),
        compiler_params=_cparams(("arbitrary",)),
        name="dsa_prompt",
    )(jnp.asarray(it), jnp.asarray(jt), jnp.asarray(ph), k_i, q_iT, w_iT, k_pad, vT_aug, qT_pad, toep)


def _merge_kernel(oa_ref, ob_ref, ga_ref, gb_ref, x_ref, wa_ref, wb_ref, wo_ref, o_ref):
    ua = jnp.dot(oa_ref[...], wa_ref[...], preferred_element_type=_F32)
    ub = jnp.dot(ob_ref[...], wb_ref[...], preferred_element_type=_F32)
    merged = jax.nn.sigmoid(ga_ref[...]) * ua + jax.nn.sigmoid(gb_ref[...]) * ub
    o_ref[...] = x_ref[...] + jnp.dot(merged.astype(_BF), wo_ref[...], preferred_element_type=_F32)


def _merge(oa, ob, ga, gb, x, wa, wb, wo, tm):
    M, D = x.shape
    W = oa.shape[1]
    rowD = pl.BlockSpec((tm, D), lambda i: (i, 0))
    rowW = pl.BlockSpec((tm, W), lambda i: (i, 0))
    const = lambda shape: pl.BlockSpec(shape, lambda i: (0, 0), pipeline_mode=pl.Buffered(1))
    return pl.pallas_call(
        _merge_kernel,
        grid=(M // tm,),
        in_specs=[rowW, rowW, rowD, rowD, rowD, const((W, D)), const((W, D)), const((D, D))],
        out_specs=rowD,
        out_shape=jax.ShapeDtypeStruct((M, D), _F32),
        compiler_params=_cparams(("arbitrary",)),
        name="merge",
    )(oa, ob, ga, gb, x, wa.astype(_BF), wb.astype(_BF), wo.astype(_BF))


def _ffn_gate_down(y, x, wd_ref, dff):
    a = y[:, :dff]
    g = y[:, dff:]
    act = (g * jax.nn.sigmoid(g)) * a
    return x + jnp.dot(act.astype(_BF), wd_ref[...], preferred_element_type=_F32)


def _ffn_prompt_kernel(x_ref, g_ref, wu_ref, cw_ref, cb_ref, wd_ref, o_ref, st_ref, u_ref, *, tm, dff):
    @pl.when(pl.program_id(0) == 0)
    def _():
        u_ref[0:8, :] = jnp.zeros((8, 2 * dff), _F32)

    x = x_ref[...]
    h = _rms(x, g_ref[...]).astype(_BF)
    u_ref[8:8 + tm, :] = jnp.dot(h, wu_ref[...], preferred_element_type=_F32)
    y = (cb_ref[...] + cw_ref[0:1, :] * u_ref[6:6 + tm, :] + cw_ref[1:2, :] * u_ref[7:7 + tm, :]
         + cw_ref[2:3, :] * u_ref[8:8 + tm, :])
    o_ref[...] = _ffn_gate_down(y, x, wd_ref, dff)
    tail = u_ref[8 + tm - 2:8 + tm, :]
    st_ref[...] = tail
    u_ref[6:8, :] = tail


def _ffn_prompt(x, gamma, w_up, conv_w, conv_b, w_down, tm=256):
    T, D = x.shape
    dff = w_down.shape[0]
    const = lambda shape: pl.BlockSpec(shape, lambda i: (0, 0), pipeline_mode=pl.Buffered(1))
    return pl.pallas_call(
        functools.partial(_ffn_prompt_kernel, tm=tm, dff=dff),
        grid=(T // tm,),
        in_specs=[pl.BlockSpec((tm, D), lambda i: (i, 0)), const((1, D)), const((D, 2 * dff)),
                  const((3, 2 * dff)), const((1, 2 * dff)), const((dff, D))],
        out_specs=[pl.BlockSpec((tm, D), lambda i: (i, 0)), pl.BlockSpec((2, 2 * dff), lambda i: (0, 0))],
        out_shape=[jax.ShapeDtypeStruct((T, D), _F32), jax.ShapeDtypeStruct((2, 2 * dff), _F32)],
        scratch_shapes=[pltpu.VMEM((tm + 8, 2 * dff), _F32)],
        compiler_params=_cparams(("arbitrary",)),
        name="ffn_prompt",
    )(x, gamma.reshape(1, D), w_up.astype(_BF), conv_w, conv_b.reshape(1, -1), w_down.astype(_BF))


def _ffn_sample_kernel(x_ref, g_ref, wu_ref, cw_ref, cb_ref, wd_ref, s0_ref, s1_ref, o_ref, u_out_ref, *, dff):
    x = x_ref[...]
    h = _rms(x, g_ref[...]).astype(_BF)
    u = jnp.dot(h, wu_ref[...], preferred_element_type=_F32)
    y = cb_ref[...] + cw_ref[0:1, :] * s0_ref[...] + cw_ref[1:2, :] * s1_ref[...] + cw_ref[2:3, :] * u
    o_ref[...] = _ffn_gate_down(y, x, wd_ref, dff)
    u_out_ref[...] = u


def _ffn_sample(x, gamma, w_up, conv_w, conv_b, w_down, state):
    B, D = x.shape
    dff = w_down.shape[0]
    return pl.pallas_call(
        functools.partial(_ffn_sample_kernel, dff=dff),
        out_shape=[jax.ShapeDtypeStruct((B, D), _F32), jax.ShapeDtypeStruct((B, 2 * dff), _F32)],
        compiler_params=pltpu.CompilerParams(vmem_limit_bytes=VMEM_LIMIT),
        name="ffn_sample",
    )(x, gamma.reshape(1, D), w_up.astype(_BF), conv_w, conv_b.reshape(1, -1), w_down.astype(_BF),
      state[:, 0, :], state[:, 1, :])


def _norm_kernel(x_ref, g_ref, o_ref):
    o_ref[...] = _rms(x_ref[...], g_ref[...])


def _final_norm(x, gamma, tm):
    M, D = x.shape
    return pl.pallas_call(
        _norm_kernel,
        grid=(M // tm,),
        in_specs=[pl.BlockSpec((tm, D), lambda i: (i, 0)), pl.BlockSpec((1, D), lambda i: (0, 0))],
        out_specs=pl.BlockSpec((tm, D), lambda i: (i, 0)),
        out_shape=jax.ShapeDtypeStruct((M, D), _F32),
        compiler_params=_cparams(("arbitrary",)),
        name="final_norm",
    )(x, gamma.reshape(1, D))


def _proj_sample_kernel(x_ref, g_ref, w_ref, o_ref):
    h = _rms(x_ref[...], g_ref[...]).astype(_BF)
    o_ref[...] = jnp.dot(h, w_ref[...], preferred_element_type=_F32)


def _proj_sample(x, gamma, w_in):
    B, D = x.shape
    n = w_in.shape[1]
    npad = -(-n // LANES) * LANES
    w = jnp.pad(w_in, ((0, 0), (0, npad - n))).astype(_BF)
    z = pl.pallas_call(
        _proj_sample_kernel,
        out_shape=jax.ShapeDtypeStruct((B, npad), _F32),
        compiler_params=pltpu.CompilerParams(vmem_limit_bytes=VMEM_LIMIT),
        name="proj_sample",
    )(x, gamma.reshape(1, D), w)
    return z[:, :n]


def _logsig_kernel(f_ref, b_ref, o_ref):
    o_ref[...] = _log_sigmoid(f_ref[...] + b_ref[...])


def _log_sigmoid_rows(f, b):
    return pl.pallas_call(
        _logsig_kernel,
        out_shape=jax.ShapeDtypeStruct(f.shape, _F32),
        name="sample_logf",
    )(f, b.reshape(1, -1))


def _pages_last(cache):
    nd = cache.ndim
    return jnp.transpose(cache, (0, 1) + tuple(range(3, nd)) + (2,))


SCORE_GROUP = 8


def _sample_scores_kernel(pt_sm, pool_ref, q8_ref, w8_ref, knew_ref, o_ref, onew_ref, buf, sem,
                          *, n_pages, n_batch, layer):
    b = pl.program_id(0)

    def page_copy(bb, p, slot):
        return pltpu.make_async_copy(pool_ref.at[layer, pt_sm[bb, p]], buf.at[slot, p], sem.at[slot])

    def start_all(bb, slot):
        def body(p, c):
            page_copy(bb, p, slot).start()
            return c
        lax.fori_loop(0, n_pages, body, 0)

    @pl.when(b == 0)
    def _():
        start_all(0, 0)

    @pl.when(b + 1 < n_batch)
    def _():
        start_all(b + 1, (b + 1) % 2)

    slot = b % 2

    def wait_body(p, c):
        page_copy(b, p, slot).wait()
        return c
    lax.fori_loop(0, n_pages, wait_body, 0)

    q8 = q8_ref[0]
    w8 = w8_ref[0]
    gsz = SCORE_GROUP
    for g in range(n_pages // gsz):
        keys = buf[slot, g * gsz:(g + 1) * gsz].reshape(gsz * IDX_DIM, PAGE_SIZE).astype(_BF)
        s = jnp.maximum(jnp.dot(q8, keys, preferred_element_type=_F32), 0.0) * w8
        o_ref[0, g * gsz:(g + 1) * gsz, :] = jnp.sum(s.reshape(gsz, IDX_HEADS, PAGE_SIZE), axis=1)
    q = q8[0:IDX_HEADS, 0:IDX_DIM]
    sn = _dot_nt(q, knew_ref[0].astype(_BF))
    onew_ref[0] = jnp.sum(jnp.maximum(sn, 0.0) * w8[0:IDX_HEADS], axis=0, keepdims=True)


def _sample_scores(cache_kidx, layer, page_table, q_idx, w_idx, k_new):
    B, NP = page_table.shape
    gsz = SCORE_GROUP
    assert NP % gsz == 0
    pool = _pages_last(cache_kidx)
    idx_scale = (IDX_DIM ** -0.5) * (IDX_HEADS ** -0.5)
    eye = jnp.eye(gsz, dtype=q_idx.dtype)
    q8 = (eye[None, :, None, :, None] * q_idx[:, None, :, None, :]).reshape(B, gsz * IDX_HEADS, gsz * IDX_DIM)
    w8 = jnp.tile(w_idx * idx_scale, (1, gsz))[:, :, None]
    knew8 = jnp.pad(k_new[:, None, :], ((0, 0), (0, 7), (0, 0)))
    per_b = lambda b, pt: (b, 0, 0)
    gs = pltpu.PrefetchScalarGridSpec(
        num_scalar_prefetch=1,
        grid=(B,),
        in_specs=[pl.BlockSpec(memory_space=pl.ANY),
                  pl.BlockSpec((1, gsz * IDX_HEADS, gsz * IDX_DIM), per_b),
                  pl.BlockSpec((1, gsz * IDX_HEADS, 1), per_b),
                  pl.BlockSpec((1, 8, IDX_DIM), per_b)],
        out_specs=[pl.BlockSpec((1, NP, PAGE_SIZE), per_b), pl.BlockSpec((1, 1, 8), per_b)],
        scratch_shapes=[pltpu.VMEM((2, NP, IDX_DIM, PAGE_SIZE), _F32), pltpu.SemaphoreType.DMA((2,))],
    )
    sc, sn = pl.pallas_call(
        functools.partial(_sample_scores_kernel, n_pages=NP, n_batch=B, layer=layer),
        grid_spec=gs,
        out_shape=[jax.ShapeDtypeStruct((B, NP, PAGE_SIZE), _F32), jax.ShapeDtypeStruct((B, 1, 8), _F32)],
        compiler_params=_cparams(("arbitrary",)),
        name="sample_scores",
    )(page_table, pool, q8.astype(_BF), w8, knew8)
    return sc, sn[:, 0, 0]


def _topk_kernel(x_ref, o_ref, buf_ref, *, k):
    B, L = x_ref.shape
    buf_ref[...] = x_ref[...]
    lane = lax.broadcasted_iota(jnp.int32, (B, L), 1).astype(_F32)

    def body(r, c):
        x = buf_ref[...]
        m = jnp.max(x, axis=1, keepdims=True)
        idx = jnp.min(jnp.where(x == m, lane, float(L)), axis=1, keepdims=True)
        buf_ref[...] = jnp.where(lane == idx, -jnp.inf, x)
        return c

    lax.fori_loop(0, k, body, 0)
    o_ref[...] = jnp.where(buf_ref[...] != x_ref[...], 1.0, 0.0)


def _topk_mask(scores, k):
    B, L = scores.shape
    return pl.pallas_call(
        functools.partial(_topk_kernel, k=k),
        out_shape=jax.ShapeDtypeStruct((B, L), _F32),
        scratch_shapes=[pltpu.VMEM((B, L), _F32)],
        compiler_params=pltpu.CompilerParams(vmem_limit_bytes=VMEM_LIMIT),
        name="sample_topk",
    )(scores)


def _bucket_edges():
    tab = _bucket_table(4 * MAX_DIST)
    max_exact = N_BUCKETS // 2
    edges = [int(np.argmax(tab >= b)) for b in range(max_exact + 1, N_BUCKETS)]
    for b, e in zip(range(max_exact + 1, N_BUCKETS), edges):
        assert tab[e] == b and tab[e - 1] == b - 1
    return edges


def _bias_rows(dist, rbT):
    max_exact = N_BUCKETS // 2
    bucket = jnp.full(dist.shape, max_exact, jnp.int32)
    for e in _bucket_edges():
        bucket = bucket + jnp.where(dist >= e, 1, 0)
    bucket = jnp.where(dist < max_exact, dist, bucket)
    onehot = (lax.broadcasted_iota(jnp.int32, (N_BUCKETS, dist.shape[1]), 0) == bucket).astype(_BF)
    return _dot3(rbT, onehot)


def _head_block_mask():
    r = lax.broadcasted_iota(jnp.int32, (N_HEADS, N_HEADS * HEAD_DIM), 0)
    c = lax.broadcasted_iota(jnp.int32, (N_HEADS, N_HEADS * HEAD_DIM), 1)
    return (c // HEAD_DIM) == r


def _paged_attn_kernel(pt_sm, kc_ref, vc_ref, *rest, forget, n_groups, group, n_batch, layer, past):
    if forget:
        (lf_ref, qbd_ref, knew_ref, vnew_ref, lfnew_ref, upper_ref,
         o_ref, kbuf, vbuf, lfbuf, sem, m_ref, l_ref, acc_ref, c_ref) = rest
    else:
        (sel_ref, selnew_ref, qbd_ref, knew_ref, vnew_ref, rbT_ref,
         o_ref, kbuf, vbuf, sem, m_ref, l_ref, acc_ref, bias_ref) = rest
    b = pl.program_id(0)
    gi = pl.program_id(1)
    n = b * n_groups + gi
    W = N_HEADS * HEAD_DIM

    def copies(bb, gg, slot):
        out = []
        for g in range(group):
            page = pt_sm[bb, gg * group + g]
            out.append(pltpu.make_async_copy(kc_ref.at[layer, page], kbuf.at[slot, g], sem.at[slot]))
            out.append(pltpu.make_async_copy(vc_ref.at[layer, page], vbuf.at[slot, g], sem.at[slot]))
            if forget:
                out.append(pltpu.make_async_copy(lf_ref.at[layer, page], lfbuf.at[slot, g], sem.at[slot]))
        return out

    @pl.when(n == 0)
    def _():
        for cp in copies(0, 0, 0):
            cp.start()

    @pl.when(n + 1 < n_batch * n_groups)
    def _():
        nxt = n + 1
        for cp in copies(nxt // n_groups, nxt % n_groups, nxt % 2):
            cp.start()

    slot = n % 2
    for cp in copies(b, gi, slot):
        cp.wait()

    @pl.when(gi == 0)
    def _():
        m_ref[...] = jnp.full(m_ref.shape, NEG, _F32)
        l_ref[...] = jnp.zeros(l_ref.shape, _F32)
        acc_ref[...] = jnp.zeros(acc_ref.shape, _F32)
        if forget:
            c_ref[...] = jnp.zeros(c_ref.shape, _F32)
        else:
            bias_ref[...] = jnp.broadcast_to(rbT_ref[:, N_BUCKETS - 1:N_BUCKETS], bias_ref.shape)

    if not forget:
        @pl.when(gi == n_groups - 1)
        def _():
            lane = lax.broadcasted_iota(jnp.int32, (1, group * PAGE_SIZE), 1)
            dist = past - (gi * group * PAGE_SIZE + lane)
            bias_ref[...] = _bias_rows(dist, rbT_ref[...])

    qbd = qbd_ref[0]
    if forget:
        upper = upper_ref[...]
        carry = c_ref[...]
    pieces = []
    for g in range(group):
        keys = kbuf[slot, g].reshape(W, PAGE_SIZE).astype(_BF)
        s = jnp.dot(qbd, keys, preferred_element_type=_F32)
        if forget:
            csum = carry + _dot3(lfbuf[slot, g], upper)
            carry = csum[:, PAGE_SIZE - 1:PAGE_SIZE]
            s = s - csum
        pieces.append(s)
    s = jnp.concatenate(pieces, axis=1)
    if forget:
        c_ref[...] = carry
    else:
        sel = sel_ref[0].reshape(1, group * PAGE_SIZE)
        s = jnp.where(sel > 0.0, s + bias_ref[...], NEG)
    m_old = m_ref[...]
    m_new = jnp.maximum(m_old, jnp.max(s, axis=1, keepdims=True))
    alpha = jnp.exp(m_old - m_new)
    pr = jnp.exp(s - m_new).astype(_BF)
    l_ref[...] = alpha * l_ref[...] + jnp.sum(pr.astype(_F32), axis=1, keepdims=True)
    pv = jnp.zeros((N_HEADS, W), _F32)
    for g in range(group):
        vals = vbuf[slot, g].reshape(W, PAGE_SIZE).astype(_BF)
        pv = pv + _dot_nt(pr[:, g * PAGE_SIZE:(g + 1) * PAGE_SIZE], vals)
    acc_ref[...] = alpha * acc_ref[...] + pv
    m_ref[...] = m_new

    @pl.when(gi == n_groups - 1)
    def _():
        s_new = jnp.sum(qbd.astype(_F32) * knew_ref[0], axis=1, keepdims=True)
        m_o = m_ref[...]
        if forget:
            s_new = s_new - (c_ref[...] + lfnew_ref[0])
            m_n = jnp.maximum(m_o, s_new)
            p_new = jnp.exp(s_new - m_n)
        else:
            picked = selnew_ref[0][:, 0:1] > 0.0
            s_new = s_new + rbT_ref[:, 0:1]
            m_n = jnp.where(picked, jnp.maximum(m_o, s_new), m_o)
            p_new = jnp.where(picked, jnp.exp(s_new - m_n), 0.0)
        a = jnp.exp(m_o - m_n)
        l_fin = a * l_ref[...] + p_new
        full = (a * acc_ref[...] + p_new * vnew_ref[0]) / l_fin
        o_ref[0] = jnp.sum(jnp.where(_head_block_mask(), full, 0.0), axis=0, keepdims=True)


def _block_diag_q(q):
    B = q.shape[0]
    qh = (q * (HEAD_DIM ** -0.5)).reshape(B, N_HEADS, 1, HEAD_DIM)
    eye = jnp.eye(N_HEADS, dtype=q.dtype)[None, :, :, None]
    return (qh * eye).reshape(B, N_HEADS, N_HEADS * HEAD_DIM).astype(_BF)


def _paged_attention(cache_k, cache_v, layer, page_table, q, k_new, v_new, *, logf=None, logf_new=None,
                     sel=None, sel_new=None, rel_bias=None, group=16):
    forget = logf is not None
    B, NP = page_table.shape
    group = min(group, NP)
    assert NP % group == 0
    n_groups = NP // group
    W = N_HEADS * HEAD_DIM
    kT = _pages_last(cache_k)
    vT = _pages_last(cache_v)
    any_spec = pl.BlockSpec(memory_space=pl.ANY)
    per_b = lambda b, g, pt: (b, 0, 0)
    common_in = [pl.BlockSpec((1, N_HEADS, W), per_b), pl.BlockSpec((1, 1, W), per_b), pl.BlockSpec((1, 1, W), per_b)]
    common_args = (_block_diag_q(q), k_new[:, None, :], v_new[:, None, :])
    page_buf = pltpu.VMEM((2, group, N_HEADS, HEAD_DIM, PAGE_SIZE), _F32)
    stats = [pltpu.VMEM((N_HEADS, 1), _F32), pltpu.VMEM((N_HEADS, 1), _F32), pltpu.VMEM((N_HEADS, W), _F32)]
    if forget:
        lane = np.arange(PAGE_SIZE)
        upper = jnp.asarray(lane[:, None] <= lane[None, :], _BF)
        in_specs = [any_spec, any_spec, any_spec] + common_in + [
            pl.BlockSpec((1, N_HEADS, 1), per_b), pl.BlockSpec((PAGE_SIZE, PAGE_SIZE), lambda b, g, pt: (0, 0))]
        args = (kT, vT, _pages_last(logf)) + common_args + (logf_new[:, :, None], upper)
        scratch = [page_buf, page_buf, pltpu.VMEM((2, group, N_HEADS, PAGE_SIZE), _F32),
                   pltpu.SemaphoreType.DMA((2,))] + stats + [pltpu.VMEM((N_HEADS, 1), _F32)]
    else:
        in_specs = [any_spec, any_spec,
                    pl.BlockSpec((1, group, PAGE_SIZE), lambda b, g, pt: (b, g, 0)),
                    pl.BlockSpec((1, 1, PAGE_SIZE), per_b)] + common_in + [
            pl.BlockSpec((N_HEADS, N_BUCKETS), lambda b, g, pt: (0, 0))]
        args = (kT, vT, sel, sel_new) + common_args + (rel_bias.T,)
        scratch = [page_buf, page_buf, pltpu.SemaphoreType.DMA((2,))] + stats + [
            pltpu.VMEM((N_HEADS, group * PAGE_SIZE), _F32)]
    gs = pltpu.PrefetchScalarGridSpec(
        num_scalar_prefetch=1,
        grid=(B, n_groups),
        in_specs=in_specs,
        out_specs=pl.BlockSpec((1, 1, W), per_b),
        scratch_shapes=scratch,
    )
    out = pl.pallas_call(
        functools.partial(_paged_attn_kernel, forget=forget, n_groups=n_groups, group=group, n_batch=B,
                          layer=layer, past=NP * PAGE_SIZE),
        grid_spec=gs,
        out_shape=jax.ShapeDtypeStruct((B, 1, W), _F32),
        compiler_params=_cparams(("arbitrary", "arbitrary")),
        name="fox_sample" if forget else "dsa_sample",
    )(page_table, *args)
    return out[:, 0, :]


def _prompt_attention_inputs(pp):
    T = pp["k_a"].shape[0]
    heads = lambda x: x.reshape(T, N_HEADS, HEAD_DIM).transpose(1, 0, 2).astype(_BF)
    pad_k = lambda k, cols: jnp.concatenate(
        [k] + cols + [jnp.zeros((N_HEADS, T, LANES - HEAD_DIM - len(cols)), _BF)], axis=-1)
    pad_q = lambda qT, n1: jnp.concatenate(
        [qT.reshape(N_HEADS, HEAD_DIM, T), jnp.ones((N_HEADS, n1, T), _BF),
         jnp.zeros((N_HEADS, LANES - HEAD_DIM - n1, T), _BF)], axis=1)
    c = _cumsum_time(pp["logfT"])
    cols = [x[:, :, None] for x in _split3(-LOG2E * c)]
    fox = (pad_k(heads(pp["k_b"]), cols), _values_with_ones(pp["v_bT"]), pad_q(pp["q_bT"], 3))
    dsa = (pad_k(heads(pp["k_a"]), []), _values_with_ones(pp["v_aT"]), pad_q(pp["q_aT"], 0))
    return fox, dsa


def kernel(x_prompt, x_sample, cache_k_a, cache_v_a, cache_kidx, cache_k_b, cache_v_b, cache_logf_b,
           state_conv, page_table, rel_bias, attn_norm, w_in, b_forget, w_a_up, w_b_up, w_o,
           ffn_norm, w_up, conv_w, conv_b, w_down, final_norm):
    depth = w_in.shape[0]
    Bp, T, D = x_prompt.shape
    Bs = x_sample.shape[0]
    assert Bp == 1 and x_sample.shape[1] == 1
    NP = page_table.shape[1]
    P = NP * PAGE_SIZE
    W = N_HEADS * HEAD_DIM
    topk_s = min(TOPK_MAX, (P + 1) // 4)
    tm = min(512, T)

    xp = x_prompt[0]
    xs = x_sample[:, 0, :]
    sizes = (W, W, W, IDX_HEADS * IDX_DIM, IDX_DIM, IDX_HEADS, W, W, W, N_HEADS, D, D)
    cuts = np.cumsum((0,) + sizes)
    new = [[] for _ in range(14)]
    for l in range(depth):
        pp = _proj_prompt(xp, attn_norm[l], w_in[l], b_forget[l])
        fox, dsa = _prompt_attention_inputs(pp)
        obT = _fox_prompt(*fox)
        oaT = _dsa_prompt(pp["k_i"][:, :IDX_DIM].astype(_BF), pp["q_iT"], pp["w_iT"], *dsa, rel_bias)
        xp = _merge(oaT.T, obT.T, pp["g_a"], pp["g_b"], xp, w_a_up[l], w_b_up[l], w_o[l], tm)
        xp, conv_p = _ffn_prompt(xp, ffn_norm[l], w_up[l], conv_w[l], conv_b[l], w_down[l])

        z = _proj_sample(xs, attn_norm[l], w_in[l])
        (q_a, k_a, v_a, q_i, k_i, w_i, q_b, k_b, v_b, f_b, g_a, g_b) = [z[:, cuts[k]:cuts[k + 1]] for k in range(12)]
        logf_s = _log_sigmoid_rows(f_b, b_forget[l])
        sc, sn = _sample_scores(cache_kidx, l, page_table, q_i.reshape(Bs, IDX_HEADS, IDX_DIM), w_i, k_i)
        pad = jnp.full((Bs, PAGE_SIZE - 1), -jnp.inf, _F32)
        scores = jnp.concatenate([sc.reshape(Bs, P), sn[:, None], pad], axis=1)
        sel = _topk_mask(scores, topk_s).reshape(Bs, NP + 1, PAGE_SIZE)
        oa_s = _paged_attention(cache_k_a, cache_v_a, l, page_table, q_a, k_a, v_a,
                                sel=sel[:, :NP], sel_new=sel[:, NP:], rel_bias=rel_bias)
        ob_s = _paged_attention(cache_k_b, cache_v_b, l, page_table, q_b, k_b, v_b,
                                logf=cache_logf_b, logf_new=logf_s)
        xs = _merge(oa_s.astype(_BF), ob_s.astype(_BF), g_a, g_b, xs, w_a_up[l], w_b_up[l], w_o[l], Bs)
        xs, u_s = _ffn_sample(xs, ffn_norm[l], w_up[l], conv_w[l], conv_b[l], w_down[l], state_conv[l])
        conv_s = jnp.stack([state_conv[l][:, 1, :], u_s], axis=1)

        hd = lambda t, n: t.reshape(t.shape[0], n, HEAD_DIM)
        rows = (hd(pp["k_a"], N_HEADS)[None], hd(k_a, N_HEADS)[:, None],
                hd(pp["v_a"], N_HEADS)[None], hd(v_a, N_HEADS)[:, None],
                pp["k_i"][None, :, :IDX_DIM], k_i[:, None],
                hd(pp["k_b"], N_HEADS)[None], hd(k_b, N_HEADS)[:, None],
                hd(pp["v_b"], N_HEADS)[None], hd(v_b, N_HEADS)[:, None],
                pp["logfT"].T[None], logf_s[:, None],
                conv_p[None], conv_s)
        for lst, r in zip(new, rows):
            lst.append(r)
    y_prompt = _final_norm(xp, final_norm, tm)[None]
    y_sample = _final_norm(xs, final_norm, Bs)[:, None]
    return (y_prompt, y_sample) + tuple(jnp.stack(n) for n in new)
```
